```python
import math
import jax, jax.numpy as jnp
from jax import lax
import numpy as np

D_MODEL = 1024
BATCH = 4
SEQ = 4096
DEPTH = 4
DEC_BATCH = 32
DEC_SEQ = 8
PAST_LEN = 8192
PAGE_SIZE = 128

HEAD_DIM = 64
N_TOK_HEADS = 12
TOK_WIDTH = N_TOK_HEADS * HEAD_DIM
N_DIFF_HEADS = N_TOK_HEADS // 2
DIFF_V_DIM = 2 * HEAD_DIM
N_MEM = 256
MEM_HEADS = 4
MEM_WIDTH = MEM_HEADS * HEAD_DIM
MIX_WIDTH = TOK_WIDTH + MEM_WIDTH
IN_WIDTH = 3 * TOK_WIDTH + MEM_WIDTH
IDX_HEADS = 8
IDX_DIM = 64
IDX_WIDTH = IDX_HEADS * IDX_DIM + IDX_DIM + IDX_HEADS
IDX_W_SCALE = IDX_HEADS ** -0.5 * IDX_DIM ** -0.5
TOPK_MAX = 256
N_MIXERS = 3
N_A_LAYERS = len(range(0, DEPTH, N_MIXERS))
N_C_LAYERS = len(range(2, DEPTH, N_MIXERS))
D_FF = -(-8 * D_MODEL // (3 * 256)) * 256
ROPE_THETA = 10000.0
Q_BLOCK = 128
LN_EPS = 1e-5
RMS_EPS = 1e-5
DEEPNORM_ALPHA = (2 * DEPTH) ** 0.25
DEEPNORM_BETA = (8 * DEPTH) ** -0.25

kernel_name = 'hybrid_dsa_stickbreak_diffattn_decoder_step'


def layer_norm(x, g, b):
    xf = x.astype(jnp.float32)
    mu = jnp.mean(xf, axis=-1, keepdims=True)
    var = jnp.mean(jnp.square(xf - mu), axis=-1, keepdims=True)
    y = (xf - mu) * lax.rsqrt(var + LN_EPS) * g.astype(jnp.float32) + b.astype(jnp.float32)
    return y.astype(x.dtype)


def rope(x, pos):
    half = x.shape[-1] // 2
    inv = ROPE_THETA ** (-jnp.arange(half, dtype=jnp.float32) / half)
    ang = pos.astype(jnp.float32)[:, None] * inv[None, :]
    shape = (1, ang.shape[0]) + (1,) * (x.ndim - 3) + (half,)
    cos = jnp.cos(ang).reshape(shape)
    sin = jnp.sin(ang).reshape(shape)
    xf = x.astype(jnp.float32)
    x1, x2 = xf[..., :half], xf[..., half:]
    return jnp.concatenate([x1 * cos - x2 * sin, x2 * cos + x1 * sin], axis=-1).astype(x.dtype)


def causal_mask(qpos, kpos, strict=False):
    if strict:
        return kpos[None, :] < qpos[:, None]
    return kpos[None, :] <= qpos[:, None]


def q_slice(a, t0):
    return lax.dynamic_slice_in_dim(a, t0, Q_BLOCK, axis=1)


def sweep_query_blocks(block_fn, n_q):
    out = lax.map(block_fn, jnp.arange(n_q // Q_BLOCK) * Q_BLOCK)
    return jnp.swapaxes(out, 0, 1).reshape(out.shape[1], n_q, out.shape[-1])


def split_heads(h):
    B, T, _ = h.shape
    q = h[..., :TOK_WIDTH].reshape(B, T, N_TOK_HEADS, HEAD_DIM)
    k = h[..., TOK_WIDTH:2 * TOK_WIDTH].reshape(B, T, N_TOK_HEADS, HEAD_DIM)
    v = h[..., 2 * TOK_WIDTH:3 * TOK_WIDTH].reshape(B, T, N_TOK_HEADS, HEAD_DIM)
    mq = h[..., 3 * TOK_WIDTH:IN_WIDTH].reshape(B, T, MEM_HEADS, HEAD_DIM)
    return q, k, v, mq


def split_indexer(h, pos):
    B, T, _ = h.shape
    e = h[..., IN_WIDTH:]
    nq = IDX_HEADS * IDX_DIM
    iq = rope(e[..., :nq].reshape(B, T, IDX_HEADS, IDX_DIM), pos)
    ik = rope(e[..., nq:nq + IDX_DIM], pos)
    iw = e[..., nq + IDX_DIM:] * IDX_W_SCALE
    return iq, ik, iw


def memory_attention(mq, mk, mv):
    s = jnp.einsum('bthd,bmhd->bhtm', mq, mk).astype(jnp.float32) / math.sqrt(HEAD_DIM)
    p = jax.nn.softmax(s, axis=-1).astype(mv.dtype)
    o = jnp.einsum('bhtm,bmhd->bthd', p, mv)
    return o.reshape(mq.shape[0], mq.shape[1], MEM_WIDTH)


def indexer_select(iq, iw, ik, qpos, kpos, n_sel):
    logits = jnp.einsum('bqhd,bsd->bqhs', iq, ik).astype(jnp.float32)
    score = jnp.einsum('bqh,bqhs->bqs', iw.astype(jnp.float32), jax.nn.relu(logits))
    score = jnp.where(causal_mask(qpos, kpos)[None], score, -jnp.inf)
    _, sel = lax.top_k(score, n_sel)
    return sel


def gathered_attention(q, kg, vg, valid):
    s = jnp.einsum('bqhd,bqkhd->bhqk', q, kg).astype(jnp.float32) / math.sqrt(HEAD_DIM)
    s = jnp.where(valid[:, None], s, -jnp.inf)
    p = jax.nn.softmax(s, axis=-1).astype(vg.dtype)
    return jnp.einsum('bhqk,bqkhd->bqhd', p, vg)


def take_rows(a, idx):
    return jax.vmap(lambda ab, ib: ab[ib])(a, idx)


def gather_selected_rows(cache_k, cache_v, layer, page_table, k_new, v_new, sel, past):
    past_pos = jnp.minimum(sel, past - 1)
    phys = take_rows(page_table, past_pos // PAGE_SIZE)
    off = past_pos % PAGE_SIZE
    new_pos = jnp.clip(sel - past, 0, k_new.shape[1] - 1)
    is_new = (sel >= past)[..., None, None]
    kg = jnp.where(is_new, take_rows(k_new, new_pos), cache_k[layer, phys, off])
    vg = jnp.where(is_new, take_rows(v_new, new_pos), cache_v[layer, phys, off])
    return kg, vg


def stick_breaking_attention(q, k, v, qpos, kpos):
    z = jnp.einsum('bqhd,bshd->bhqs', q, k).astype(jnp.float32) / math.sqrt(HEAD_DIM)
    strict = causal_mask(qpos, kpos, strict=True)
    log_stay = jnp.where(strict, jax.nn.log_sigmoid(-z), 0.0)
    rc = lax.cumsum(log_stay, axis=3, reverse=True)
    after = jnp.concatenate([rc[..., 1:], jnp.zeros_like(rc[..., :1])], axis=-1)
    w = jnp.where(strict, jnp.exp(jax.nn.log_sigmoid(z) + after), 0.0).astype(v.dtype)
    return jnp.einsum('bhqs,bshd->bqhd', w, v)


def differential_attention(q, k, v, qpos, kpos, lam, lam_init, norm_g):
    B, Q = q.shape[:2]
    S = k.shape[1]
    q2 = q.reshape(B, Q, N_DIFF_HEADS, 2, HEAD_DIM)
    k2 = k.reshape(B, S, N_DIFF_HEADS, 2, HEAD_DIM)
    s = jnp.einsum('bqhcd,bshcd->bchqs', q2, k2).astype(jnp.float32) / math.sqrt(HEAD_DIM)
    s = jnp.where(causal_mask(qpos, kpos), s, -jnp.inf)
    p = jax.nn.softmax(s, axis=-1)
    a = (p[:, 0] - lam * p[:, 1]).astype(v.dtype)
    o = jnp.einsum('bhqs,bshe->bqhe', a, v.reshape(B, S, N_DIFF_HEADS, DIFF_V_DIM)).astype(jnp.float32)
    o = o * lax.rsqrt(jnp.mean(o * o, axis=-1, keepdims=True) + RMS_EPS)
    o = o * norm_g.astype(jnp.float32) * (1.0 - lam_init)
    return o.astype(v.dtype)


def swiglu(x, w_gu, w_d):
    h = jnp.einsum('btd,df->btf', x, w_gu)
    g, u = h[..., :D_FF], h[..., D_FF:]
    return jnp.einsum('btf,fd->btd', jax.nn.silu(g) * u, w_d)


def finish_layer(x, tok, mem, w_o_l, g1, b1, w_gu, w_d, g2, b2):
    mix = jnp.einsum('bte,ed->btd', jnp.concatenate([tok, mem], axis=-1), w_o_l)
    x = layer_norm(DEEPNORM_ALPHA * x + mix, g1, b1)
    return layer_norm(DEEPNORM_ALPHA * x + swiglu(x, w_gu, w_d), g2, b2)


def setup_inputs(seed: int = 0) -> dict:
    key = jax.random.key(seed)
    ks = jax.random.split(key, 24)
    f32 = jnp.float32
    n_pages = PAST_LEN // PAGE_SIZE
    n_used = DEC_BATCH * n_pages
    n_pool = n_used + n_used // 4

    def nrm(k, shape, scale=1.0):
        return jax.random.normal(k, shape, f32) * scale

    page_table = jax.random.permutation(ks[0], n_pool)[:n_used].reshape(DEC_BATCH, n_pages).astype(jnp.int32)
    return {
        'x_prompt': nrm(ks[1], (BATCH, SEQ, D_MODEL)),
        'x_sample': nrm(ks[2], (DEC_BATCH, DEC_SEQ, D_MODEL)),
        'mem_prompt': nrm(ks[3], (BATCH, N_MEM, D_MODEL)),
        'cache_k': nrm(ks[4], (DEPTH, n_pool, PAGE_SIZE, N_TOK_HEADS, HEAD_DIM)),
        'cache_v': nrm(ks[5], (DEPTH, n_pool, PAGE_SIZE, N_TOK_HEADS, HEAD_DIM)),
        'cache_idx_k': nrm(ks[6], (N_A_LAYERS, n_pool, PAGE_SIZE, IDX_DIM)),
        'cache_mem_k': nrm(ks[7], (DEPTH, DEC_BATCH, N_MEM, MEM_HEADS, HEAD_DIM)),
        'cache_mem_v': nrm(ks[8], (DEPTH, DEC_BATCH, N_MEM, MEM_HEADS, HEAD_DIM)),
        'page_table': page_table,
        'w_in': nrm(ks[9], (DEPTH, D_MODEL, IN_WIDTH), D_MODEL ** -0.5),
        'w_idx': nrm(ks[10], (N_A_LAYERS, D_MODEL, IDX_WIDTH), D_MODEL ** -0.5),
        'w_mem_kv': nrm(ks[11], (DEPTH, D_MODEL, 2 * MEM_WIDTH), D_MODEL ** -0.5),
        'w_o': nrm(ks[12], (DEPTH, MIX_WIDTH, D_MODEL), MIX_WIDTH ** -0.5 * DEEPNORM_BETA),
        'ln_mix_g': 1.0 + nrm(ks[13], (DEPTH, D_MODEL), 0.02),
        'ln_mix_b': nrm(ks[14], (DEPTH, D_MODEL), 0.02),
        'w_gate_up': nrm(ks[15], (DEPTH, D_MODEL, 2 * D_FF), D_MODEL ** -0.5),
        'w_down': nrm(ks[16], (DEPTH, D_FF, D_MODEL), D_FF ** -0.5 * DEEPNORM_BETA),
        'ln_ffn_g': 1.0 + nrm(ks[17], (DEPTH, D_MODEL), 0.02),
        'ln_ffn_b': nrm(ks[18], (DEPTH, D_MODEL), 0.02),
        'diff_lambda': nrm(ks[19], (N_C_LAYERS, 4, HEAD_DIM), 0.1),
        'diff_norm_g': 1.0 + nrm(ks[20], (N_C_LAYERS, DIFF_V_DIM), 0.02),
    }


def reference(x_prompt, x_sample, mem_prompt, cache_k, cache_v, cache_idx_k, cache_mem_k, cache_mem_v,
              page_table, w_in, w_idx, w_mem_kv, w_o, ln_mix_g, ln_mix_b, w_gate_up, w_down,
              ln_ffn_g, ln_ffn_b, diff_lambda, diff_norm_g):
    Bp, Tp, _ = x_prompt.shape
    Bs, Ts, _ = x_sample.shape
    n_pages = page_table.shape[1]
    past = n_pages * PAGE_SIZE
    pos_p = jnp.arange(Tp)
    pos_s = past + jnp.arange(Ts)
    kpos_s = jnp.arange(past + Ts)
    nsel_p = min(TOPK_MAX, Tp // 4)
    nsel_s = min(TOPK_MAX, (past + Ts) // 4)

    xp, xs = x_prompt, x_sample
    nk_p, nv_p, nidx_p, nmk_p, nmv_p = [], [], [], [], []
    nk_s, nv_s, nidx_s = [], [], []
    for l in range(DEPTH):
        kind = l % N_MIXERS
        slot = l // N_MIXERS
        if kind == 0:
            w = jnp.concatenate([w_in[l], w_idx[slot]], axis=1)
        else:
            w = w_in[l]
        hp = jnp.einsum('btd,de->bte', xp, w)
        hs = jnp.einsum('btd,de->bte', xs, w)
        qp, kp, vp, mqp = split_heads(hp)
        qs, ks, vs, mqs = split_heads(hs)
        if kind != 1:
            qp, kp = rope(qp, pos_p), rope(kp, pos_p)
            qs, ks = rope(qs, pos_s), rope(ks, pos_s)
        nk_p.append(kp)
        nv_p.append(vp)
        nk_s.append(ks)
        nv_s.append(vs)

        mkv = jnp.einsum('bmd,de->bme', mem_prompt, w_mem_kv[l])
        mkp = mkv[..., :MEM_WIDTH].reshape(Bp, N_MEM, MEM_HEADS, HEAD_DIM)
        mvp = mkv[..., MEM_WIDTH:].reshape(Bp, N_MEM, MEM_HEADS, HEAD_DIM)
        nmk_p.append(mkp)
        nmv_p.append(mvp)
        mem_p = memory_attention(mqp, mkp, mvp)
        mem_s = memory_attention(mqs, cache_mem_k[l], cache_mem_v[l])

        if kind == 0:
            iqp, ikp, iwp = split_indexer(hp, pos_p)
            iqs, iks, iws = split_indexer(hs, pos_s)
            nidx_p.append(ikp)
            nidx_s.append(iks)

            def blk_a(t0):
                qpos = t0 + jnp.arange(Q_BLOCK)
                sel = indexer_select(q_slice(iqp, t0), q_slice(iwp, t0), ikp, qpos, pos_p, nsel_p)
                valid = sel <= qpos[None, :, None]
                o = gathered_attention(q_slice(qp, t0), take_rows(kp, sel), take_rows(vp, sel), valid)
                return o.reshape(Bp, Q_BLOCK, TOK_WIDTH)
            tok_p = sweep_query_blocks(blk_a, Tp)

            ik_past = cache_idx_k[slot, page_table].reshape(Bs, past, IDX_DIM)
            ik_all = jnp.concatenate([ik_past, iks], axis=1)
            sel = indexer_select(iqs, iws, ik_all, pos_s, kpos_s, nsel_s)
            kg, vg = gather_selected_rows(cache_k, cache_v, l, page_table, ks, vs, sel, past)
            valid = sel <= pos_s[None, :, None]
            tok_s = gathered_attention(qs, kg, vg, valid).reshape(Bs, Ts, TOK_WIDTH)
        else:
            k_all = jnp.concatenate(
                [cache_k[l, page_table].reshape(Bs, past, N_TOK_HEADS, HEAD_DIM), ks], axis=1)
            v_all = jnp.concatenate(
                [cache_v[l, page_table].reshape(Bs, past, N_TOK_HEADS, HEAD_DIM), vs], axis=1)
            if kind == 1:
                def blk_b(t0):
                    qpos = t0 + jnp.arange(Q_BLOCK)
                    o = stick_breaking_attention(q_slice(qp, t0), kp, vp, qpos, pos_p)
                    return o.reshape(Bp, Q_BLOCK, TOK_WIDTH)
                tok_p = sweep_query_blocks(blk_b, Tp)
                tok_s = stick_breaking_attention(qs, k_all, v_all, pos_s, kpos_s).reshape(Bs, Ts, TOK_WIDTH)
            else:
                lam_init = 0.8 - 0.6 * math.exp(-0.3 * l)
                lp = diff_lambda[slot].astype(jnp.float32)
                lam = jnp.exp(jnp.sum(lp[0] * lp[1])) - jnp.exp(jnp.sum(lp[2] * lp[3])) + lam_init
                g = diff_norm_g[slot]

                def blk_c(t0):
                    qpos = t0 + jnp.arange(Q_BLOCK)
                    o = differential_attention(q_slice(qp, t0), kp, vp, qpos, pos_p, lam, lam_init, g)
                    return o.reshape(Bp, Q_BLOCK, TOK_WIDTH)
                tok_p = sweep_query_blocks(blk_c, Tp)
                tok_s = differential_attention(qs, k_all, v_all, pos_s, kpos_s, lam, lam_init, g
                                               ).reshape(Bs, Ts, TOK_WIDTH)

        xp = finish_layer(xp, tok_p, mem_p, w_o[l], ln_mix_g[l], ln_mix_b[l], w_gate_up[l], w_down[l],
                          ln_ffn_g[l], ln_ffn_b[l])
        xs = finish_layer(xs, tok_s, mem_s, w_o[l], ln_mix_g[l], ln_mix_b[l], w_gate_up[l], w_down[l],
                          ln_ffn_g[l], ln_ffn_b[l])

    new_k_prompt = jnp.stack(nk_p)
    new_v_prompt = jnp.stack(nv_p)
    new_idx_k_prompt = jnp.stack(nidx_p)
    new_mem_k_prompt = jnp.stack(nmk_p)
    new_mem_v_prompt = jnp.stack(nmv_p)
    new_k_sample = jnp.stack(nk_s)
    new_v_sample = jnp.stack(nv_s)
    new_idx_k_sample = jnp.stack(nidx_s)
    return (xp, xs, new_k_prompt, new_v_prompt, new_idx_k_prompt, new_mem_k_prompt, new_mem_v_prompt,
            new_k_sample, new_v_sample, new_idx_k_sample)
```

```python
import functools
import math

import jax
import jax.numpy as jnp
from jax import lax
from jax.experimental import pallas as pl
from jax.experimental.pallas import tpu as pltpu

F32 = jnp.float32
BF16 = jnp.bfloat16
I32 = jnp.int32

HEAD_DIM = 64
TOK_WIDTH = 768
N_PAIRS = 6
MEM_WIDTH = 256
IDX_HEADS = 8
IDX_DIM = 64
IDX_Q_WIDTH = IDX_HEADS * IDX_DIM
IDX_W_SCALE = IDX_HEADS ** -0.5 * IDX_DIM ** -0.5
PAGE_SIZE = 128
TOPK_MAX = 256
N_MIXERS = 3
ROPE_THETA = 10000.0
LN_EPS = 1e-5
RMS_EPS = 1e-5
QK_SCALE = HEAD_DIM ** -0.5

LANES = 128
NEG = -1e30
INT_MIN = -2 ** 31
MIB = 1024 * 1024


def _tiles(rows):
    return dict(proj=min(512, rows), ffn=min(512, rows), attn=min(256, rows))


def _cparams(sem, vmem_mib):
    return pltpu.CompilerParams(dimension_semantics=sem, vmem_limit_bytes=vmem_mib * MIB)


def _dot(a, b):
    return jnp.dot(a, b, preferred_element_type=F32)


def _dot_nt(a, b):
    return lax.dot_general(a, b, (((1,), (1,)), ((), ())), preferred_element_type=F32)


def _lane_iota():
    return lax.broadcasted_iota(I32, (1, LANES), 1)


def _layer_norm(y, g, b):
    mu = jnp.mean(y, axis=-1, keepdims=True)
    d = y - mu
    var = jnp.mean(d * d, axis=-1, keepdims=True)
    return d * lax.rsqrt(var + LN_EPS) * g + b


def _softmax_update(s, m, l, acc, pv):
    m_new = jnp.maximum(m, jnp.max(s, axis=-1, keepdims=True))
    a = jnp.exp(m - m_new)
    p = jnp.exp(s - m_new)
    l = a * l + jnp.sum(p, axis=-1, keepdims=True)
    acc = a * acc + pv(p.astype(BF16))
    return m_new, l, acc


def _sort_key(x):
    bits = lax.bitcast_convert_type(x, I32)
    return bits ^ ((bits >> 31) & 0x7FFFFFFF)


def _softplus(z):
    return jnp.maximum(z, 0.0) + jnp.log(1.0 + jnp.exp(-jnp.abs(z)))


def _diff_lambda(lp, lam_init):
    return (jnp.exp(jnp.sum(lp[0:1] * lp[1:2], axis=-1, keepdims=True))
            - jnp.exp(jnp.sum(lp[2:3] * lp[3:4], axis=-1, keepdims=True)) + lam_init)


def _head_split(q):
    lo_half = _lane_iota() < HEAD_DIM
    zero = jnp.zeros_like(q)
    return jnp.where(lo_half, q, zero), jnp.where(lo_half, zero, q)


def _proj_kernel(*refs, n_rope_chunks, outs, tk):
    if n_rope_chunks:
        x_ref, w_ref, cos_ref, sin_ref = refs[:4]
        o_refs = refs[4:]
    else:
        x_ref, w_ref = refs[:2]
        o_refs = refs[2:]
    y = _dot(x_ref[...].astype(BF16), w_ref[...])
    n_chunks = y.shape[1] // LANES
    chunks = [y[:, c * LANES:(c + 1) * LANES] for c in range(n_chunks)]
    if n_rope_chunks:
        cos = cos_ref[...]
        sin = sin_ref[...]
        first = (_lane_iota() & (HEAD_DIM - 1)) < HEAD_DIM // 2
        for c in range(n_rope_chunks):
            yc = chunks[c]
            partner = jnp.where(first, pltpu.roll(yc, LANES - HEAD_DIM // 2, 1),
                                pltpu.roll(yc, HEAD_DIM // 2, 1))
            chunks[c] = yc * cos + partner * sin
    for o_ref, (kind, lo, hi, scale) in zip(o_refs, outs):
        c0, c1 = lo // LANES, -(-hi // LANES)
        if kind == "pair":
            for p in range(c1 - c0):
                v = chunks[c0 + p]
                o_ref[p] = (v * scale if scale != 1.0 else v).astype(o_ref.dtype)
            continue
        v = chunks[c0] if c1 - c0 == 1 else jnp.concatenate(chunks[c0:c1], axis=1)
        if scale != 1.0:
            v = v * scale
        if kind == "flat":
            o_ref[...] = v[:, lo - c0 * LANES:hi - c0 * LANES].astype(o_ref.dtype)
        elif kind == "T":
            o_ref[...] = v.T[0:hi - lo].astype(o_ref.dtype)
        else:
            vt = v.T.astype(o_ref.dtype)
            for j in range(vt.shape[1] // tk):
                o_ref[j] = vt[:, j * tk:(j + 1) * tk]


def _proj(x, w, outs, *, batch, rope=None, n_rope_chunks=0, name):
    m_rows, k_dim = x.shape
    n_cols = w.shape[1]
    t_len = m_rows // batch
    structured = any(o[0] != "flat" for o in outs)
    tm = _tiles(t_len if structured else m_rows)["proj"]
    tk = _tiles(t_len)["attn"]
    tiles_per_seq = max(t_len // tm, 1)
    in_specs = [pl.BlockSpec((tm, k_dim), lambda i: (i, 0)),
                pl.BlockSpec((k_dim, n_cols), lambda i: (0, 0))]
    args = [x, w]
    if n_rope_chunks:
        cos, sin = rope
        n_tab = cos.shape[0] // tm
        in_specs += [pl.BlockSpec((tm, LANES), lambda i: (i % n_tab, 0))] * 2
        args += [cos, sin]
    seq_of = lambda i: i // tiles_per_seq
    tile_of = lambda i: i % tiles_per_seq
    out_shapes, out_specs = [], []
    for kind, lo, hi, _, dtype in outs:
        if kind == "pair":
            groups = (hi - lo) // LANES
            shape, block = (batch, groups, t_len, LANES), (None, groups, tm, LANES)
            imap = lambda i: (seq_of(i), 0, tile_of(i), 0)
        elif kind == "T":
            shape, block = (batch, hi - lo, t_len), (None, hi - lo, tm)
            imap = lambda i: (seq_of(i), 0, tile_of(i))
        elif kind == "Ttile":
            shape, block = (batch, t_len // tk, hi - lo, tk), (None, tm // tk, hi - lo, tk)
            imap = lambda i: (seq_of(i), tile_of(i), 0, 0)
        else:
            shape, block = (m_rows, hi - lo), (tm, hi - lo)
            imap = lambda i: (i, 0)
        out_shapes.append(jax.ShapeDtypeStruct(shape, dtype))
        out_specs.append(pl.BlockSpec(block, imap))
    kern = functools.partial(_proj_kernel, n_rope_chunks=n_rope_chunks,
                             outs=tuple(o[:4] for o in outs), tk=tk)
    return pl.pallas_call(
        kern,
        grid=(m_rows // tm,),
        in_specs=in_specs,
        out_specs=out_specs,
        out_shape=out_shapes,
        compiler_params=_cparams(("parallel",), 56),
        name=name,
    )(*args)


def _mem_attn_kernel(q_ref, kt_ref, vt_ref, o_ref):
    lo_half = _lane_iota() < HEAD_DIM
    pieces = []
    for pr in range(MEM_WIDTH // LANES):
        sl = slice(pr * LANES, (pr + 1) * LANES)
        kt = kt_ref[sl, :].astype(BF16)
        vt = vt_ref[sl, :].astype(BF16)
        res = []
        for qc in _head_split(q_ref[:, sl].astype(BF16)):
            s = _dot(qc, kt)
            m = jnp.max(s, axis=-1, keepdims=True)
            p = jnp.exp(s - m)
            l = jnp.sum(p, axis=-1, keepdims=True)
            res.append(_dot_nt(p.astype(BF16), vt) / l)
        pieces.append(jnp.where(lo_half, res[0], res[1]))
    o_ref[...] = jnp.concatenate(pieces, axis=1).astype(o_ref.dtype)


def _mem_attn(q, kt, vt, layer, out_dtype, name):
    b, t, _ = q.shape
    tq = _tiles(t)["attn"]
    n_mem = kt.shape[-1]
    if layer is None:
        kv_spec = pl.BlockSpec((None, MEM_WIDTH, n_mem), lambda bi, i: (bi, 0, 0))
    else:
        kv_spec = pl.BlockSpec((None, None, MEM_WIDTH, n_mem), lambda bi, i: (layer, bi, 0, 0))
    return pl.pallas_call(
        _mem_attn_kernel,
        grid=(b, t // tq),
        in_specs=[pl.BlockSpec((None, tq, MEM_WIDTH), lambda bi, i: (bi, i, 0)), kv_spec, kv_spec],
        out_specs=pl.BlockSpec((None, tq, MEM_WIDTH), lambda bi, i: (bi, i, 0)),
        out_shape=jax.ShapeDtypeStruct((b, t, MEM_WIDTH), out_dtype),
        compiler_params=_cparams(("parallel", "parallel"), 32),
        name=name,
    )(q, kt, vt)


def _tile_iotas(t):
    return (lax.broadcasted_iota(I32, (t, t), 0), lax.broadcasted_iota(I32, (t, t), 1))


def _attn_c_kernel(lam_ref, g_ref, q_ref, kt_ref, v_ref, o_ref, *, tq, lam_init):
    i = pl.program_id(2)
    qs = _head_split(q_ref[...])
    row, col = _tile_iotas(tq)
    causal = col <= row

    def tile(kt, carry, diag):
        k = kt_ref[kt]
        v = v_ref[pl.ds(pl.multiple_of(kt * tq, tq), tq), :]
        new = []
        for c in range(2):
            s = _dot(qs[c], k)
            if diag:
                s = jnp.where(causal, s, NEG)
            new.append(_softmax_update(s, *carry[c], lambda p: _dot(p, v)))
        return tuple(new)

    one = (jnp.full((tq, 1), NEG, F32), jnp.zeros((tq, 1), F32), jnp.zeros((tq, LANES), F32))
    carry = lax.fori_loop(0, i, lambda kt, c: tile(kt, c, False), (one, one))
    (_, l0, a0), (_, l1, a1) = tile(i, carry, True)
    o = a0 / l0 - _diff_lambda(lam_ref[...], lam_init) * (a1 / l1)
    o = o * lax.rsqrt(jnp.mean(o * o, axis=-1, keepdims=True) + RMS_EPS)
    o_ref[...] = (o * g_ref[...] * (1.0 - lam_init)).astype(o_ref.dtype)


def _attn_b_kernel(u_ref, q_ref, kt_ref, v_ref, o_ref, *, tq):
    i = pl.program_id(2)
    qs = _head_split(q_ref[...])
    u = u_ref[...]
    row, col = _tile_iotas(tq)
    strict = col < row

    def tile(kt, carry, diag):
        k = kt_ref[kt]
        v = v_ref[pl.ds(pl.multiple_of(kt * tq, tq), tq), :]
        new = []
        for c in range(2):
            later, acc = carry[c]
            z = _dot(qs[c], k)
            sp = _softplus(z)
            spm = jnp.where(strict, sp, 0.0) if diag else sp
            cs = _dot(spm.astype(BF16), u)
            w = jnp.exp(z - sp - (cs + later))
            if diag:
                w = jnp.where(strict, w, 0.0)
            acc = acc + _dot(w.astype(BF16), v)
            later = later + cs[:, 0:1] + spm[:, 0:1]
            new.append((later, acc))
        return tuple(new)

    one = (jnp.zeros((tq, 1), F32), jnp.zeros((tq, LANES), F32))
    carry = tile(i, (one, one), True)
    carry = lax.fori_loop(0, i, lambda n, c: tile(i - 1 - n, c, False), carry)
    lo_half = _lane_iota() < HEAD_DIM
    o_ref[...] = jnp.where(lo_half, carry[0][1], carry[1][1]).astype(o_ref.dtype)


def _fold_lanes(c):
    out = c[:, 0:LANES]
    for j in range(1, c.shape[1] // LANES):
        out = out + c[:, j * LANES:(j + 1) * LANES]
    return out


def _attn_a_kernel(l_ref, iq_ref, ikt_ref, iw_ref, q_ref, kt_ref, v_ref, o_ref, key_scr, bias_scr,
                   *, tq, n_sel):
    i = pl.program_id(1)
    pair = pl.program_id(2)
    row, col = _tile_iotas(tq)
    causal = col <= row

    @pl.when(pair == 0)
    def _select():
        w_all = iw_ref[...]
        iq_heads = []
        for pr in range(IDX_Q_WIDTH // LANES):
            iq_heads += list(_head_split(iq_ref[:, pr * LANES:(pr + 1) * LANES]))

        def score_tile(kt, diag):
            ik = ikt_ref[kt]
            sc = jnp.zeros((tq, tq), F32)
            for h in range(IDX_HEADS):
                sc = sc + w_all[:, h:h + 1] * jnp.maximum(_dot(iq_heads[h], ik), 0.0)
            key = _sort_key(sc)
            if diag:
                key = jnp.where(causal, key, INT_MIN)
            key_scr[kt] = key

        def _score_body(kt, c):
            score_tile(kt, False)
            return c

        lax.fori_loop(0, i, _score_body, 0)
        score_tile(i, True)

        def count(pred):
            def body(kt, acc):
                return acc + _fold_lanes(jnp.where(pred(key_scr[kt]), 1.0, 0.0))
            acc = lax.fori_loop(0, i + 1, body, jnp.zeros((tq, LANES), F32))
            return jnp.sum(acc, axis=-1, keepdims=True)

        def search(it, cur):
            cand = cur + jnp.left_shift(jnp.int32(1), 31 - it)
            cnt = count(lambda key: key >= cand)
            return jnp.where(cnt >= n_sel, cand, cur)

        cur = lax.fori_loop(0, 32, search, jnp.full((tq, 1), INT_MIN, I32))
        thr = jnp.maximum(cur, INT_MIN + 1)
        need = n_sel - count(lambda key: key > thr)
        lmat = l_ref[...]

        def select(kt, seen):
            key = key_scr[kt]
            eq = key == thr
            eqf = jnp.where(eq, 1.0, 0.0)
            rank = _dot(eqf.astype(BF16), lmat) + seen
            sel = (key > thr) | (eq & (rank < need))
            bias_scr[kt] = jnp.where(sel, 0.0, NEG)
            return seen + jnp.sum(eqf, axis=-1, keepdims=True)

        lax.fori_loop(0, i + 1, select, jnp.zeros((tq, 1), F32))

    qs = _head_split(q_ref[...])

    def tile(kt, carry):
        k = kt_ref[kt]
        v = v_ref[pl.ds(pl.multiple_of(kt * tq, tq), tq), :]
        bias = bias_scr[kt]
        return tuple(_softmax_update(_dot(qs[c], k) + bias, *carry[c], lambda p: _dot(p, v))
                     for c in range(2))

    one = (jnp.full((tq, 1), NEG, F32), jnp.zeros((tq, 1), F32), jnp.zeros((tq, LANES), F32))
    (_, l0, a0), (_, l1, a1) = lax.fori_loop(0, i + 1, tile, (one, one))
    lo_half = _lane_iota() < HEAD_DIM
    o_ref[...] = jnp.where(lo_half, a0 / l0, a1 / l1).astype(o_ref.dtype)


def _strict_lower(n):
    r = lax.broadcasted_iota(I32, (n, n), 0)
    c = lax.broadcasted_iota(I32, (n, n), 1)
    return (r > c).astype(BF16)


def _prompt_mixer(kind, q, kt, v, *, extra, name):
    b, _, t, _ = q.shape
    tq = _tiles(t)["attn"]
    nq = t // tq
    out_shape = jax.ShapeDtypeStruct((b, t, TOK_WIDTH), BF16)
    if kind == 0:
        iq, ikt, iw, n_sel = extra
        lmat = _strict_lower(tq).T
        kern = functools.partial(_attn_a_kernel, tq=tq, n_sel=float(n_sel))
        return pl.pallas_call(
            kern,
            grid=(b, nq, N_PAIRS),
            in_specs=[
                pl.BlockSpec((tq, tq), lambda bi, i, p: (0, 0)),
                pl.BlockSpec((None, tq, IDX_Q_WIDTH), lambda bi, i, p: (bi, i, 0)),
                pl.BlockSpec((None, nq, LANES, tq), lambda bi, i, p: (bi, 0, 0, 0)),
                pl.BlockSpec((None, tq, LANES), lambda bi, i, p: (bi, i, 0)),
                pl.BlockSpec((None, None, tq, LANES), lambda bi, i, p: (bi, p, i, 0)),
                pl.BlockSpec((None, nq, LANES, tq), lambda bi, i, p: (bi, 0, p, 0)),
                pl.BlockSpec((None, None, t, LANES), lambda bi, i, p: (bi, p, 0, 0)),
            ],
            out_specs=pl.BlockSpec((None, tq, LANES), lambda bi, i, p: (bi, i, p)),
            out_shape=out_shape,
            scratch_shapes=[pltpu.VMEM((nq, tq, tq), I32), pltpu.VMEM((nq, tq, tq), F32)],
            compiler_params=_cparams(("parallel", "parallel", "arbitrary"), 48),
            name=name,
        )(lmat, iq, ikt, iw, q, kt, v)
    qkv_specs = [
        pl.BlockSpec((None, None, tq, LANES), lambda bi, p, i: (bi, p, i, 0)),
        pl.BlockSpec((None, nq, LANES, tq), lambda bi, p, i: (bi, 0, p, 0)),
        pl.BlockSpec((None, None, t, LANES), lambda bi, p, i: (bi, p, 0, 0)),
    ]
    out_spec = pl.BlockSpec((None, tq, LANES), lambda bi, p, i: (bi, i, p))
    if kind == 1:
        kern = functools.partial(_attn_b_kernel, tq=tq)
        consts = [_strict_lower(tq)]
        const_specs = [pl.BlockSpec((tq, tq), lambda bi, p, i: (0, 0))]
    else:
        lam_p, norm_g, lam_init = extra
        kern = functools.partial(_attn_c_kernel, tq=tq, lam_init=lam_init)
        consts = [lam_p, norm_g.reshape(1, LANES)]
        const_specs = [pl.BlockSpec(lam_p.shape, lambda bi, p, i: (0, 0)),
                       pl.BlockSpec((1, LANES), lambda bi, p, i: (0, 0))]
    return pl.pallas_call(
        kern,
        grid=(b, N_PAIRS, nq),
        in_specs=const_specs + qkv_specs,
        out_specs=out_spec,
        out_shape=out_shape,
        compiler_params=_cparams(("parallel", "parallel", "parallel"), 32),
        name=name,
    )(*consts, q, kt, v)


def _dec_idx_kernel(pt_ref, iq_ref, w_ref, ikn_ref, l_ref, *rest, pages, groups, n_sel):
    page_refs = rest[:pages]
    bias_ref = rest[pages]
    key_scr = rest[pages + 1]
    g = pl.program_id(1)
    ts = bias_ref.shape[1]
    width = pages * PAGE_SIZE
    iq = iq_ref[...]
    w = w_ref[...]

    def to_keys(logits):
        lg = jnp.maximum(logits, 0.0) * w
        s = lg[0:ts]
        for h in range(1, IDX_HEADS):
            s = s + lg[h * ts:(h + 1) * ts]
        return _sort_key(s)

    @pl.when(g == 0)
    def _new_keys():
        ikn = jnp.concatenate([ikn_ref[...], jnp.zeros((PAGE_SIZE - ts, IDX_DIM), F32)], axis=0)
        key = to_keys(_dot_nt(iq, ikn.astype(BF16)))
        t_idx = lax.broadcasted_iota(I32, (ts, PAGE_SIZE), 0)
        j_idx = lax.broadcasted_iota(I32, (ts, PAGE_SIZE), 1)
        key = jnp.where(j_idx <= t_idx, key, INT_MIN)
        key_scr[0] = jnp.concatenate(
            [key, jnp.full((ts, width - PAGE_SIZE), INT_MIN, I32)], axis=1)

    key_scr[g + 1] = jnp.concatenate(
        [to_keys(_dot(iq, r[...].astype(BF16))) for r in page_refs], axis=1)

    @pl.when(g == groups - 1)
    def _select():
        def count(pred):
            acc = jnp.zeros((ts, width), F32)
            for sl in range(groups + 1):
                acc = acc + jnp.where(pred(key_scr[sl]), 1.0, 0.0)
            return jnp.sum(acc, axis=-1, keepdims=True)

        def search(it, cur):
            cand = cur + jnp.left_shift(jnp.int32(1), 31 - it)
            return jnp.where(count(lambda key: key >= cand) >= n_sel, cand, cur)

        cur = lax.fori_loop(0, 32, search, jnp.full((ts, 1), INT_MIN, I32))
        thr = jnp.maximum(cur, INT_MIN + 1)
        need = n_sel - count(lambda key: key > thr)
        lmat = l_ref[...]
        seen = jnp.zeros((ts, 1), F32)
        for sl in list(range(1, groups + 1)) + [0]:
            key = key_scr[sl]
            eq = key == thr
            eqf = jnp.where(eq, 1.0, 0.0)
            eq16 = jnp.concatenate([eqf, jnp.zeros_like(eqf)], axis=0).astype(BF16)
            rank = _dot(eq16, lmat)[0:ts] + seen
            sel = (key > thr) | (eq & (rank < need))
            bias_ref[sl] = jnp.where(sel, 0.0, NEG)
            seen = seen + jnp.sum(eqf, axis=-1, keepdims=True)


def _dec_attn_kernel(pt_ref, *refs, mode, pages, groups, ts, lam_init):
    it = iter(refs)
    qbd_ref, kn_ref, vn_ref = next(it), next(it), next(it)
    bias_ref = next(it) if mode == "a" else None
    u_ref = next(it) if mode == "b" else None
    lam_ref, gn_ref = (next(it), next(it)) if mode == "c" else (None, None)
    kt_refs = [next(it) for _ in range(pages)]
    vt_refs = [next(it) for _ in range(pages)]
    o_ref = next(it)
    m_scr, l_scr, acc_scr = next(it), next(it), next(it)
    g = pl.program_id(1)
    rows = 2 * N_PAIRS * ts
    qbd = qbd_ref[...]

    def rows_of(x):
        return jnp.concatenate([x] * (2 * N_PAIRS), axis=0)

    def step(s, pvs, valid, u):
        if mode == "b":
            sp = _softplus(s)
            spm = sp if valid is None else jnp.where(valid, sp, 0.0)
            cs = _dot(spm.astype(BF16), u)
            w = jnp.exp(s - sp - (cs + m_scr[...]))
            if valid is not None:
                w = jnp.where(valid, w, 0.0)
            m_scr[...] += cs[:, 0:1] + spm[:, 0:1]
            p = w.astype(BF16)
            acc = acc_scr[...]
        else:
            if valid is not None:
                s = jnp.where(valid, s, NEG)
            m_old = m_scr[...]
            m_new = jnp.maximum(m_old, jnp.max(s, axis=-1, keepdims=True))
            a = jnp.exp(m_old - m_new)
            pf = jnp.exp(s - m_new)
            l_scr[...] = a * l_scr[...] + jnp.sum(pf, axis=-1, keepdims=True)
            m_scr[...] = m_new
            p = pf.astype(BF16)
            acc = a * acc_scr[...]
        for j, pv in enumerate(pvs):
            acc = acc + pv(p[:, j * PAGE_SIZE:(j + 1) * PAGE_SIZE])
        acc_scr[...] = acc

    @pl.when(g == 0)
    def _new_keys():
        m_scr[...] = jnp.full((rows, 1), 0.0 if mode == "b" else NEG, F32)
        l_scr[...] = jnp.zeros((rows, 1), F32)
        acc_scr[...] = jnp.zeros((rows, TOK_WIDTH), F32)
        pad = jnp.zeros((PAGE_SIZE - ts, TOK_WIDTH), F32)
        kn = jnp.concatenate([kn_ref[...], pad], axis=0).astype(BF16)
        vn = jnp.concatenate([vn_ref[...], pad], axis=0).astype(BF16)
        s = _dot_nt(qbd, kn)
        t_idx = rows_of(lax.broadcasted_iota(I32, (ts, PAGE_SIZE), 0))
        j_idx = lax.broadcasted_iota(I32, (rows, PAGE_SIZE), 1)
        valid = (j_idx < t_idx) if mode == "b" else (j_idx <= t_idx)
        if mode == "a":
            s = s + rows_of(bias_ref[:, 0:PAGE_SIZE])
        step(s, [lambda p: _dot(p, vn)], valid,
             u_ref[0:PAGE_SIZE, 0:PAGE_SIZE] if mode == "b" else None)

    @pl.when(g > 0)
    def _pages():
        s = jnp.concatenate([_dot(qbd, r[...].astype(BF16)) for r in kt_refs], axis=1)
        if mode == "a":
            s = s + rows_of(bias_ref[...])
        pvs = [functools.partial(lambda p, r: _dot_nt(p, r[...].astype(BF16)), r=r) for r in vt_refs]
        step(s, pvs, None, u_ref[...] if mode == "b" else None)

    @pl.when(g == groups)
    def _finish():
        acc = acc_scr[...]
        if mode != "b":
            acc = acc / l_scr[...]
        lo_half = _lane_iota() < HEAD_DIM
        pieces = []
        for p in range(N_PAIRS):
            blk = acc[2 * p * ts:(2 * p + 2) * ts, p * LANES:(p + 1) * LANES]
            first, second = blk[0:ts], blk[ts:2 * ts]
            if mode == "c":
                o = first - _diff_lambda(lam_ref[...], lam_init) * second
                o = o * lax.rsqrt(jnp.mean(o * o, axis=-1, keepdims=True) + RMS_EPS)
                pieces.append(o * gn_ref[...] * (1.0 - lam_init))
            else:
                pieces.append(jnp.where(lo_half, first, second))
        o_ref[...] = jnp.concatenate(pieces, axis=1)


def _pages_per_step(n_pages):
    p = 8
    while n_pages % p:
        p //= 2
    return p


def _dec_indexer(page_table, cache_ikt, slot, iq_rows, w_rows, ik_new, n_sel, name):
    bs, n_pages = page_table.shape
    ts = ik_new.shape[1]
    pages = _pages_per_step(n_pages)
    groups = n_pages // pages
    width = pages * PAGE_SIZE
    lmat = _strict_lower(width).T

    def page_spec(p):
        return pl.BlockSpec((None, None, IDX_DIM, PAGE_SIZE),
                            lambda b, g, pt: (slot, pt[b, g * pages + p], 0, 0))

    grid_spec = pltpu.PrefetchScalarGridSpec(
        num_scalar_prefetch=1,
        grid=(bs, groups),
        in_specs=[
            pl.BlockSpec((None, IDX_HEADS * ts, IDX_DIM), lambda b, g, pt: (b, 0, 0)),
            pl.BlockSpec((None, IDX_HEADS * ts, 1), lambda b, g, pt: (b, 0, 0)),
            pl.BlockSpec((None, ts, IDX_DIM), lambda b, g, pt: (b, 0, 0)),
            pl.BlockSpec((width, width), lambda b, g, pt: (0, 0)),
        ] + [page_spec(p) for p in range(pages)],
        out_specs=pl.BlockSpec((None, groups + 1, ts, width), lambda b, g, pt: (b, 0, 0, 0)),
        scratch_shapes=[pltpu.VMEM((groups + 1, ts, width), I32)],
    )
    kern = functools.partial(_dec_idx_kernel, pages=pages, groups=groups, n_sel=float(n_sel))
    return pl.pallas_call(
        kern,
        grid_spec=grid_spec,
        out_shape=jax.ShapeDtypeStruct((bs, groups + 1, ts, width), F32),
        compiler_params=_cparams(("parallel", "arbitrary"), 32),
        name=name,
    )(page_table, iq_rows, w_rows, ik_new, lmat, *([cache_ikt] * pages))


def _dec_mixer(mode, layer, page_table, cache_kt, cache_vt, qbd, k_new, v_new, *, extra, name):
    bs, n_pages = page_table.shape
    ts = k_new.shape[1]
    pages = _pages_per_step(n_pages)
    groups = n_pages // pages
    width = pages * PAGE_SIZE
    rows = 2 * N_PAIRS * ts
    descending = mode == "b"

    def page_spec(p):
        def imap(b, g, pt):
            grp = jnp.maximum(g - 1, 0)
            if descending:
                grp = groups - 1 - grp
            return (layer, pt[b, grp * pages + p], 0, 0)
        return pl.BlockSpec((None, None, TOK_WIDTH, PAGE_SIZE), imap)

    in_specs = [
        pl.BlockSpec((None, rows, TOK_WIDTH), lambda b, g, pt: (b, 0, 0)),
        pl.BlockSpec((None, ts, TOK_WIDTH), lambda b, g, pt: (b, 0, 0)),
        pl.BlockSpec((None, ts, TOK_WIDTH), lambda b, g, pt: (b, 0, 0)),
    ]
    args = [qbd, k_new, v_new]
    lam_init = 0.0
    if mode == "a":
        in_specs.append(pl.BlockSpec((None, None, ts, width), lambda b, g, pt: (b, g, 0, 0)))
        args.append(extra)
    elif mode == "b":
        in_specs.append(pl.BlockSpec((width, width), lambda b, g, pt: (0, 0)))
        args.append(_strict_lower(width))
    else:
        lam_p, norm_g, lam_init = extra
        in_specs += [pl.BlockSpec(lam_p.shape, lambda b, g, pt: (0, 0)),
                     pl.BlockSpec((1, LANES), lambda b, g, pt: (0, 0))]
        args += [lam_p, norm_g.reshape(1, LANES)]
    in_specs += [page_spec(p) for p in range(pages)] * 2
    args += [cache_kt] * pages + [cache_vt] * pages
    grid_spec = pltpu.PrefetchScalarGridSpec(
        num_scalar_prefetch=1,
        grid=(bs, groups + 1),
        in_specs=in_specs,
        out_specs=pl.BlockSpec((None, ts, TOK_WIDTH), lambda b, g, pt: (b, 0, 0)),
        scratch_shapes=[pltpu.VMEM((rows, 1), F32), pltpu.VMEM((rows, 1), F32),
                        pltpu.VMEM((rows, TOK_WIDTH), F32)],
    )
    kern = functools.partial(_dec_attn_kernel, mode=mode, pages=pages, groups=groups, ts=ts,
                             lam_init=lam_init)
    return pl.pallas_call(
        kern,
        grid_spec=grid_spec,
        out_shape=jax.ShapeDtypeStruct((bs, ts, TOK_WIDTH), F32),
        compiler_params=_cparams(("parallel", "arbitrary"), 48),
        name=name,
    )(page_table, *args)


def _oproj_ln_kernel(tok_ref, mem_ref, x_ref, wt_ref, wm_ref, g_ref, b_ref, o_ref, obf_ref, *, alpha):
    mix = (_dot(tok_ref[...].astype(BF16), wt_ref[...])
           + _dot(mem_ref[...].astype(BF16), wm_ref[...]))
    out = _layer_norm(alpha * x_ref[...] + mix, g_ref[...], b_ref[...])
    o_ref[...] = out
    obf_ref[...] = out.astype(BF16)


def _oproj_ln(tok, mem, x, w_tok, w_mem, g, b, alpha, name):
    m_rows, d = x.shape
    tm = _tiles(m_rows)["proj"]
    row = lambda n: pl.BlockSpec((tm, n), lambda i: (i, 0))
    full = lambda a: pl.BlockSpec(a.shape, lambda i: (0, 0))
    return pl.pallas_call(
        functools.partial(_oproj_ln_kernel, alpha=alpha),
        grid=(m_rows // tm,),
        in_specs=[row(TOK_WIDTH), row(MEM_WIDTH), row(d), full(w_tok), full(w_mem), full(g), full(b)],
        out_specs=[row(d), row(d)],
        out_shape=[jax.ShapeDtypeStruct((m_rows, d), F32), jax.ShapeDtypeStruct((m_rows, d), BF16)],
        compiler_params=_cparams(("parallel",), 40),
        name=name,
    )(tok, mem, x, w_tok, w_mem, g, b)


def _ffn_ln_kernel(xbf_ref, x_ref, wg_ref, wu_ref, wd_ref, g_ref, b_ref, o_ref, obf_ref, acc_ref,
                   *, alpha, n_f):
    f = pl.program_id(1)
    xb = xbf_ref[...]
    hg = _dot(xb, wg_ref[...])
    hu = _dot(xb, wu_ref[...])
    act = hg * (1.0 / (1.0 + jnp.exp(-hg))) * hu
    part = _dot(act.astype(BF16), wd_ref[...])

    @pl.when(f == 0)
    def _first():
        acc_ref[...] = part

    @pl.when(f > 0)
    def _rest():
        acc_ref[...] += part

    @pl.when(f == n_f - 1)
    def _finish():
        out = _layer_norm(alpha * x_ref[...] + acc_ref[...], g_ref[...], b_ref[...])
        o_ref[...] = out
        obf_ref[...] = out.astype(BF16)


def _ffn_ln(xbf, x, w_gu, w_d, g, b, alpha, name):
    m_rows, d = x.shape
    d_ff = w_d.shape[0]
    tm = _tiles(m_rows)["ffn"]
    n_f = 2
    tf = d_ff // n_f
    row = lambda: pl.BlockSpec((tm, d), lambda i, f: (i, 0))
    vec = lambda a: pl.BlockSpec(a.shape, lambda i, f: (0, 0))
    return pl.pallas_call(
        functools.partial(_ffn_ln_kernel, alpha=alpha, n_f=n_f),
        grid=(m_rows // tm, n_f),
        in_specs=[row(), row(),
                  pl.BlockSpec((d, tf), lambda i, f: (0, f)),
                  pl.BlockSpec((d, tf), lambda i, f: (0, f + n_f)),
                  pl.BlockSpec((tf, d), lambda i, f: (f, 0)),
                  vec(g), vec(b)],
        out_specs=[row(), row()],
        out_shape=[jax.ShapeDtypeStruct((m_rows, d), F32), jax.ShapeDtypeStruct((m_rows, d), BF16)],
        scratch_shapes=[pltpu.VMEM((tm, d), F32)],
        compiler_params=_cparams(("parallel", "arbitrary"), 56),
        name=name,
    )(xbf, x, w_gu, w_gu, w_d, g, b)


def _rope_tables(pos):
    half = HEAD_DIM // 2
    inv = ROPE_THETA ** (-jnp.arange(half, dtype=F32) / half)
    ang = pos.astype(F32)[:, None] * inv[None, :]
    cos = jnp.cos(ang)
    sin = jnp.sin(ang)
    return jnp.tile(cos, (1, 4)), jnp.tile(jnp.concatenate([-sin, sin], axis=1), (1, 2))


def _heads_last(xt, n_heads):
    l, b, _, p = xt.shape
    return xt.reshape(l, b, n_heads, HEAD_DIM, p).transpose(0, 1, 4, 2, 3)


def kernel(x_prompt, x_sample, mem_prompt, cache_k, cache_v, cache_idx_k, cache_mem_k, cache_mem_v, page_table, w_in, w_idx, w_mem_kv, w_o, ln_mix_g, ln_mix_b, w_gate_up, w_down, ln_ffn_g, ln_ffn_b, diff_lambda, diff_norm_g):
    bp, tp, d_model = x_prompt.shape
    bs, ts, _ = x_sample.shape
    n_pages = page_table.shape[1]
    past = n_pages * PAGE_SIZE
    depth = w_in.shape[0]
    n_mem = mem_prompt.shape[1]
    n_pool = cache_k.shape[1]
    n_heads = 2 * N_PAIRS
    mem_heads = MEM_WIDTH // HEAD_DIM
    alpha = (2 * depth) ** 0.25
    nsel_p = min(TOPK_MAX, tp // 4)
    nsel_s = min(TOPK_MAX, (past + ts) // 4)
    mp, ms = bp * tp, bs * ts

    w_in_b = w_in.astype(BF16)
    w_mkv_b = w_mem_kv.astype(BF16)
    w_o_b = w_o.astype(BF16)
    w_gu_b = w_gate_up.astype(BF16)
    w_d_b = w_down.astype(BF16)
    iq_w, ik_w, ih_w = (w_idx[..., :IDX_Q_WIDTH], w_idx[..., IDX_Q_WIDTH:IDX_Q_WIDTH + IDX_DIM],
                        w_idx[..., IDX_Q_WIDTH + IDX_DIM:])
    w_idx_b = jnp.concatenate(
        [iq_w, ik_w, ik_w, ih_w, jnp.zeros(ih_w.shape[:2] + (LANES - IDX_HEADS,), F32)],
        axis=-1).astype(BF16)
    idx_cols = IDX_Q_WIDTH + 2 * LANES

    rope_p = _rope_tables(jnp.arange(tp))
    rope_s = _rope_tables(jnp.tile(past + jnp.arange(ts), bs))
    cache_kt = cache_k.transpose(0, 1, 3, 4, 2).reshape(depth, n_pool, TOK_WIDTH, PAGE_SIZE)
    cache_vt = cache_v.transpose(0, 1, 3, 4, 2).reshape(depth, n_pool, TOK_WIDTH, PAGE_SIZE)
    cache_ikt = cache_idx_k.transpose(0, 1, 3, 2)
    cache_mkt = cache_mem_k.transpose(0, 1, 3, 4, 2).reshape(depth, bs, MEM_WIDTH, n_mem)
    cache_mvt = cache_mem_v.transpose(0, 1, 3, 4, 2).reshape(depth, bs, MEM_WIDTH, n_mem)
    mem2d = mem_prompt.reshape(bp * n_mem, d_model)
    head_of_col = jnp.arange(TOK_WIDTH) // HEAD_DIM
    bd_mask = (jnp.arange(n_heads)[:, None, None] == head_of_col[None, None, :])

    xp = x_prompt.reshape(mp, d_model)
    xs = x_sample.reshape(ms, d_model)
    xp_in, xs_in = xp, xs
    nk_p, nv_p, nidx_p, nmk_p, nmv_p, nk_s, nv_s, nidx_s = [], [], [], [], [], [], [], []

    for l in range(depth):
        kind = l % N_MIXERS
        slot = l // N_MIXERS
        n_rope = n_heads if kind != 1 else 0
        tag = f"l{l}"
        c_q, c_k, c_v, c_m = 0, TOK_WIDTH, 2 * TOK_WIDTH, 3 * TOK_WIDTH
        in_w = c_m + MEM_WIDTH

        q_p, kt_f, kt_p, vt_f, v_p, mq_p = _proj(
            xp_in, w_in_b[l],
            (("pair", c_q, c_k, QK_SCALE, BF16), ("T", c_k, c_v, 1.0, F32),
             ("Ttile", c_k, c_v, 1.0, BF16), ("T", c_v, c_m, 1.0, F32),
             ("pair", c_v, c_m, 1.0, BF16), ("flat", c_m, in_w, QK_SCALE, BF16)),
            batch=bp, rope=rope_p, n_rope_chunks=n_rope, name=f"proj_p_{tag}")
        nk_p.append(kt_f)
        nv_p.append(vt_f)
        mkt_f, mvt_f = _proj(
            mem2d, w_mkv_b[l],
            (("T", 0, MEM_WIDTH, 1.0, F32), ("T", MEM_WIDTH, 2 * MEM_WIDTH, 1.0, F32)),
            batch=bp, name=f"proj_mem_{tag}")
        nmk_p.append(mkt_f)
        nmv_p.append(mvt_f)
        mem_p = _mem_attn(mq_p.reshape(bp, tp, MEM_WIDTH), mkt_f, mvt_f, None, BF16, f"mem_p_{tag}")

        q_s, k_s, v_s, mq_s = _proj(
            xs_in, w_in_b[l],
            (("flat", c_q, c_k, QK_SCALE, F32), ("flat", c_k, c_v, 1.0, F32),
             ("flat", c_v, c_m, 1.0, F32), ("flat", c_m, in_w, QK_SCALE, F32)),
            batch=bs, rope=rope_s, n_rope_chunks=n_rope, name=f"proj_s_{tag}")
        k_s3 = k_s.reshape(bs, ts, TOK_WIDTH)
        v_s3 = v_s.reshape(bs, ts, TOK_WIDTH)
        nk_s.append(k_s3.reshape(bs, ts, n_heads, HEAD_DIM))
        nv_s.append(v_s3.reshape(bs, ts, n_heads, HEAD_DIM))
        qbd = jnp.where(bd_mask[None], q_s.reshape(bs, 1, ts, TOK_WIDTH), 0.0)
        qbd = qbd.reshape(bs, n_heads * ts, TOK_WIDTH).astype(BF16)
        mem_s = _mem_attn(mq_s.reshape(bs, ts, MEM_WIDTH), cache_mkt, cache_mvt, l, F32,
                          f"mem_s_{tag}")

        if kind == 0:
            iq_p, ikt_p, ikt_f, iw_p = _proj(
                xp_in, w_idx_b[slot],
                (("flat", 0, IDX_Q_WIDTH, 1.0, BF16),
                 ("Ttile", IDX_Q_WIDTH, IDX_Q_WIDTH + LANES, 1.0, BF16),
                 ("T", IDX_Q_WIDTH, IDX_Q_WIDTH + IDX_DIM, 1.0, F32),
                 ("flat", IDX_Q_WIDTH + LANES, idx_cols, IDX_W_SCALE, F32)),
                batch=bp, rope=rope_p, n_rope_chunks=IDX_Q_WIDTH // LANES + 1, name=f"proj_ip_{tag}")
            nidx_p.append(ikt_f)
            tok_p = _prompt_mixer(
                0, q_p, kt_p, v_p,
                extra=(iq_p.reshape(bp, tp, IDX_Q_WIDTH), ikt_p, iw_p.reshape(bp, tp, LANES), nsel_p),
                name=f"mix_a_{tag}")

            iq_s, ik_s, iw_s = _proj(
                xs_in, w_idx_b[slot],
                (("flat", 0, IDX_Q_WIDTH, 1.0, F32),
                 ("flat", IDX_Q_WIDTH, IDX_Q_WIDTH + IDX_DIM, 1.0, F32),
                 ("flat", IDX_Q_WIDTH + LANES, idx_cols, IDX_W_SCALE, F32)),
                batch=bs, rope=rope_s, n_rope_chunks=IDX_Q_WIDTH // LANES + 1, name=f"proj_is_{tag}")
            ik_s3 = ik_s.reshape(bs, ts, IDX_DIM)
            nidx_s.append(ik_s3)
            iq_rows = iq_s.reshape(bs, ts, IDX_HEADS, IDX_DIM).swapaxes(1, 2)
            iq_rows = iq_rows.reshape(bs, IDX_HEADS * ts, IDX_DIM).astype(BF16)
            w_rows = iw_s.reshape(bs, ts, LANES)[:, :, :IDX_HEADS].swapaxes(1, 2)
            w_rows = w_rows.reshape(bs, IDX_HEADS * ts, 1)
            bias = _dec_indexer(page_table, cache_ikt, slot, iq_rows, w_rows, ik_s3, nsel_s,
                                f"idx_s_{tag}")
            tok_s = _dec_mixer("a", l, page_table, cache_kt, cache_vt, qbd, k_s3, v_s3,
                               extra=bias, name=f"mix_sa_{tag}")
        elif kind == 1:
            tok_p = _prompt_mixer(1, q_p, kt_p, v_p, extra=None, name=f"mix_b_{tag}")
            tok_s = _dec_mixer("b", l, page_table, cache_kt, cache_vt, qbd, k_s3, v_s3,
                               extra=None, name=f"mix_sb_{tag}")
        else:
            lam_init = 0.8 - 0.6 * math.exp(-0.3 * l)
            extra = (diff_lambda[slot], diff_norm_g[slot], lam_init)
            tok_p = _prompt_mixer(2, q_p, kt_p, v_p, extra=extra, name=f"mix_c_{tag}")
            tok_s = _dec_mixer("c", l, page_table, cache_kt, cache_vt, qbd, k_s3, v_s3,
                               extra=extra, name=f"mix_sc_{tag}")

        w_tok, w_mem = w_o_b[l, :TOK_WIDTH], w_o_b[l, TOK_WIDTH:]
        g1, b1 = ln_mix_g[l].reshape(1, d_model), ln_mix_b[l].reshape(1, d_model)
        g2, b2 = ln_ffn_g[l].reshape(1, d_model), ln_ffn_b[l].reshape(1, d_model)
        x1, x1b = _oproj_ln(tok_p.reshape(mp, TOK_WIDTH), mem_p.reshape(mp, MEM_WIDTH), xp,
                            w_tok, w_mem, g1, b1, alpha, f"oproj_p_{tag}")
        xp, xp_in = _ffn_ln(x1b, x1, w_gu_b[l], w_d_b[l], g2, b2, alpha, f"ffn_p_{tag}")
        y1, y1b = _oproj_ln(tok_s.reshape(ms, TOK_WIDTH), mem_s.reshape(ms, MEM_WIDTH), xs,
                            w_tok, w_mem, g1, b1, alpha, f"oproj_s_{tag}")
        xs, xs_in = _ffn_ln(y1b, y1, w_gu_b[l], w_d_b[l], g2, b2, alpha, f"ffn_s_{tag}")

    return (xp.reshape(bp, tp, d_model), xs.reshape(bs, ts, d_model),
            _heads_last(jnp.stack(nk_p), n_heads), _heads_last(jnp.stack(nv_p), n_heads),
            jnp.stack(nidx_p).transpose(0, 1, 3, 2),
            _heads_last(jnp.stack(nmk_p), mem_heads), _heads_last(jnp.stack(nmv_p), mem_heads),
            jnp.stack(nk_s), jnp.stack(nv_s), jnp.stack(nidx_s))
```

```python
import functools
import math

import jax
import jax.numpy as jnp
from jax import lax
from jax.experimental import pallas as pl
from jax.experimental.pallas import tpu as pltpu

F32 = jnp.float32
BF16 = jnp.bfloat16
I32 = jnp.int32

HEAD_DIM = 64
TOK_WIDTH = 768
N_PAIRS = 6
MEM_WIDTH = 256
IDX_HEADS = 8
IDX_DIM = 64
IDX_Q_WIDTH = IDX_HEADS * IDX_DIM
IDX_W_SCALE = IDX_HEADS ** -0.5 * IDX_DIM ** -0.5
PAGE_SIZE = 128
TOPK_MAX = 256
N_MIXERS = 3
ROPE_THETA = 10000.0
LN_EPS = 1e-5
RMS_EPS = 1e-5
QK_SCALE = HEAD_DIM ** -0.5

LANES = 128
NEG = -1e30
STICK_EXIT = 110.0
INT_MIN = -2 ** 31
MIB = 1024 * 1024


def _tiles(rows):
    attn = min(256, rows)
    return dict(proj=min(512, rows), ffn=min(512, rows), attn=attn, chunk=min(4, rows // attn))


def _cparams(sem, vmem_mib):
    return pltpu.CompilerParams(dimension_semantics=sem, vmem_limit_bytes=vmem_mib * MIB)


def _dot(a, b):
    return jnp.dot(a, b, preferred_element_type=F32)


def _dot_nt(a, b):
    return lax.dot_general(a, b, (((1,), (1,)), ((), ())), preferred_element_type=F32)


def _lane_iota():
    return lax.broadcasted_iota(I32, (1, LANES), 1)


def _layer_norm(y, g, b):
    mu = jnp.mean(y, axis=-1, keepdims=True)
    d = y - mu
    var = jnp.mean(d * d, axis=-1, keepdims=True)
    return d * lax.rsqrt(var + LN_EPS) * g + b


def _softmax_update(s, m, l, acc, pv):
    m_new = jnp.maximum(m, jnp.max(s, axis=-1, keepdims=True))
    a = jnp.exp(m - m_new)
    p = jnp.exp(s - m_new)
    l = a * l + jnp.sum(p, axis=-1, keepdims=True)
    acc = a * acc + pv(p.astype(BF16))
    return m_new, l, acc


def _sort_key(x):
    bits = lax.bitcast_convert_type(x, I32)
    return bits ^ ((bits >> 31) & 0x7FFFFFFF)


def _softplus(z):
    return jnp.maximum(z, 0.0) + jnp.log(1.0 + jnp.exp(-jnp.abs(z)))


def _diff_lambda(lp, lam_init):
    return (jnp.exp(jnp.sum(lp[0:1] * lp[1:2], axis=-1, keepdims=True))
            - jnp.exp(jnp.sum(lp[2:3] * lp[3:4], axis=-1, keepdims=True)) + lam_init)


def _head_split(q):
    lo_half = _lane_iota() < HEAD_DIM
    zero = jnp.zeros_like(q)
    return jnp.where(lo_half, q, zero), jnp.where(lo_half, zero, q)


def _proj_kernel(*refs, n_rope_chunks, outs, tk):
    x_ref, w_ref = refs[:2]
    if n_rope_chunks:
        cos_ref, sin_ref = refs[2:4]
    o_refs = refs[len(refs) - len(outs):]
    y = _dot(x_ref[...].astype(BF16), w_ref[...])
    n_chunks = y.shape[1] // LANES
    chunks = [y[:, c * LANES:(c + 1) * LANES] for c in range(n_chunks)]
    if n_rope_chunks:
        cos = cos_ref[...]
        sin = sin_ref[...]
        first = (_lane_iota() & (HEAD_DIM - 1)) < HEAD_DIM // 2
        for c in range(n_rope_chunks):
            yc = chunks[c]
            partner = jnp.where(first, pltpu.roll(yc, LANES - HEAD_DIM // 2, 1),
                                pltpu.roll(yc, HEAD_DIM // 2, 1))
            chunks[c] = yc * cos + partner * sin
    for o_ref, (kind, lo, hi, scale) in zip(o_refs, outs):
        c0, c1 = lo // LANES, -(-hi // LANES)
        if kind == "pair":
            for p in range(c1 - c0):
                v = chunks[c0 + p]
                o_ref[p] = (v * scale if scale != 1.0 else v).astype(o_ref.dtype)
            continue
        v = chunks[c0] if c1 - c0 == 1 else jnp.concatenate(chunks[c0:c1], axis=1)
        if scale != 1.0:
            v = v * scale
        if kind == "flat":
            o_ref[...] = v[:, lo - c0 * LANES:hi - c0 * LANES].astype(o_ref.dtype)
        elif kind in ("T", "Tstack"):
            o_ref[...] = v.T[0:hi - lo].astype(o_ref.dtype)
        else:
            vt = v.T.astype(o_ref.dtype)
            for j in range(vt.shape[1] // tk):
                o_ref[j] = vt[:, j * tk:(j + 1) * tk]


def _proj(x, w, outs, *, batch, rope=None, n_rope_chunks=0, stack=None, name):
    m_rows, k_dim = x.shape
    n_cols = w.shape[1]
    t_len = m_rows // batch
    structured = any(o[0] != "flat" for o in outs)
    tm = _tiles(t_len if structured else m_rows)["proj"]
    tk = _tiles(t_len)["attn"]
    tiles_per_seq = max(t_len // tm, 1)
    in_specs = [pl.BlockSpec((tm, k_dim), lambda i: (i, 0)),
                pl.BlockSpec((k_dim, n_cols), lambda i: (0, 0))]
    args = [x, w]
    if n_rope_chunks:
        cos, sin = rope
        n_tab = cos.shape[0] // tm
        in_specs += [pl.BlockSpec((tm, LANES), lambda i: (i % n_tab, 0))] * 2
        args += [cos, sin]
    seq_of = lambda i: i // tiles_per_seq
    tile_of = lambda i: i % tiles_per_seq
    out_shapes, out_specs, aliases = [], [], {}
    for o_idx, (kind, lo, hi, _, dtype) in enumerate(outs):
        if kind == "Tstack":
            depth, layer, prev = stack
            shape, block = (depth, batch, hi - lo, t_len), (None, None, hi - lo, tm)
            imap = lambda i: (layer, seq_of(i), 0, tile_of(i))
            if prev is not None:
                aliases[len(args)] = o_idx
                in_specs.append(pl.BlockSpec(memory_space=pl.ANY))
                args.append(prev[sum(o[0] == "Tstack" for o in outs[:o_idx])])
        elif kind == "pair":
            groups = (hi - lo) // LANES
            shape, block = (batch, groups, t_len, LANES), (None, groups, tm, LANES)
            imap = lambda i: (seq_of(i), 0, tile_of(i), 0)
        elif kind == "T":
            shape, block = (batch, hi - lo, t_len), (None, hi - lo, tm)
            imap = lambda i: (seq_of(i), 0, tile_of(i))
        elif kind == "Ttile":
            shape, block = (batch, t_len // tk, hi - lo, tk), (None, tm // tk, hi - lo, tk)
            imap = lambda i: (seq_of(i), tile_of(i), 0, 0)
        else:
            shape, block = (m_rows, hi - lo), (tm, hi - lo)
            imap = lambda i: (i, 0)
        out_shapes.append(jax.ShapeDtypeStruct(shape, dtype))
        out_specs.append(pl.BlockSpec(block, imap))
    kern = functools.partial(_proj_kernel, n_rope_chunks=n_rope_chunks,
                             outs=tuple(o[:4] for o in outs), tk=tk)
    return pl.pallas_call(
        kern,
        grid=(m_rows // tm,),
        in_specs=in_specs,
        out_specs=out_specs,
        out_shape=out_shapes,
        input_output_aliases=aliases,
        compiler_params=_cparams(("parallel",), 56),
        name=name,
    )(*args)


def _mem_attn_kernel(q_ref, kt_ref, vt_ref, o_ref):
    lo_half = _lane_iota() < HEAD_DIM
    pieces = []
    for pr in range(MEM_WIDTH // LANES):
        sl = slice(pr * LANES, (pr + 1) * LANES)
        kt = kt_ref[sl, :].astype(BF16)
        vt = vt_ref[sl, :].astype(BF16)
        res = []
        for qc in _head_split(q_ref[:, sl].astype(BF16)):
            s = _dot(qc, kt)
            m = jnp.max(s, axis=-1, keepdims=True)
            p = jnp.exp(s - m)
            l = jnp.sum(p, axis=-1, keepdims=True)
            res.append(_dot_nt(p.astype(BF16), vt) / l)
        pieces.append(jnp.where(lo_half, res[0], res[1]))
    o_ref[...] = jnp.concatenate(pieces, axis=1).astype(o_ref.dtype)


def _mem_attn(q, kt, vt, layer, out_dtype, name):
    b, t, _ = q.shape
    tq = _tiles(t)["attn"]
    n_mem = kt.shape[-1]
    if layer is None:
        kv_spec = pl.BlockSpec((None, MEM_WIDTH, n_mem), lambda bi, i: (bi, 0, 0))
    else:
        kv_spec = pl.BlockSpec((None, None, MEM_WIDTH, n_mem), lambda bi, i: (layer, bi, 0, 0))
    return pl.pallas_call(
        _mem_attn_kernel,
        grid=(b, t // tq),
        in_specs=[pl.BlockSpec((None, tq, MEM_WIDTH), lambda bi, i: (bi, i, 0)), kv_spec, kv_spec],
        out_specs=pl.BlockSpec((None, tq, MEM_WIDTH), lambda bi, i: (bi, i, 0)),
        out_shape=jax.ShapeDtypeStruct((b, t, MEM_WIDTH), out_dtype),
        compiler_params=_cparams(("parallel", "parallel"), 32),
        name=name,
    )(q, kt, vt)


def _tile_iotas(t):
    return (lax.broadcasted_iota(I32, (t, t), 0), lax.broadcasted_iota(I32, (t, t), 1))


def _attn_c_kernel(lam_ref, g_ref, q_ref, kt_ref, v_ref, o_ref, *, tq, ct, lam_init):
    i = pl.program_id(2)
    diag_chunk = i // ct
    qs = _head_split(q_ref[...])

    row, col = _tile_iotas(tq)

    def chunk(c, carry, diag):
        carry = list(carry)
        for j in range(ct):
            kt = c * ct + j
            k = kt_ref[kt]
            v = v_ref[pl.ds(pl.multiple_of(kt * tq, tq), tq), :]
            for h in range(2):
                s = _dot(qs[h], k)
                if diag:
                    s = jnp.where(col + kt * tq <= row + i * tq, s, NEG)
                carry[h] = _softmax_update(s, *carry[h], lambda p: _dot(p, v))
        return tuple(carry)

    one = (jnp.full((tq, 1), NEG, F32), jnp.zeros((tq, 1), F32), jnp.zeros((tq, LANES), F32))
    carry = lax.fori_loop(0, diag_chunk, lambda c, cr: chunk(c, cr, False), (one, one))
    (_, l0, a0), (_, l1, a1) = chunk(diag_chunk, carry, True)
    o = a0 / l0 - _diff_lambda(lam_ref[...], lam_init) * (a1 / l1)
    o = o * lax.rsqrt(jnp.mean(o * o, axis=-1, keepdims=True) + RMS_EPS)
    o_ref[...] = (o * g_ref[...] * (1.0 - lam_init)).astype(o_ref.dtype)


def _attn_b_kernel(u_ref, q_ref, kt_ref, v_ref, o_ref, *, tq):
    i = pl.program_id(2)
    qs = _head_split(q_ref[...])
    u = u_ref[...]
    row, col = _tile_iotas(tq)
    strict = col < row

    def tile(kt, carry, diag):
        k = kt_ref[kt]
        v = v_ref[pl.ds(pl.multiple_of(kt * tq, tq), tq), :]
        new = []
        for c in range(2):
            later, acc = carry[c]
            z = _dot(qs[c], k)
            sp = _softplus(z)
            spm = jnp.where(strict, sp, 0.0) if diag else sp
            cs = _dot(spm.astype(BF16), u)
            w = jnp.exp(z - sp - (cs + later))
            if diag:
                w = jnp.where(strict, w, 0.0)
            acc = acc + _dot(w.astype(BF16), v)
            later = later + cs[:, 0:1] + spm[:, 0:1]
            new.append((later, acc))
        return tuple(new)

    def live(carry):
        return (jnp.minimum(jnp.min(carry[0][0]), jnp.min(carry[1][0])) < STICK_EXIT).astype(I32)

    def cond(state):
        return (state[0] >= 0) & (state[1] > 0)

    def body(state):
        carry = tile(state[0], state[2], False)
        return state[0] - 1, live(carry), carry

    one = (jnp.zeros((tq, 1), F32), jnp.zeros((tq, LANES), F32))
    carry = tile(i, (one, one), True)
    carry = lax.while_loop(cond, body, (i - 1, live(carry), carry))[2]
    lo_half = _lane_iota() < HEAD_DIM
    o_ref[...] = jnp.where(lo_half, carry[0][1], carry[1][1]).astype(o_ref.dtype)


def _fold_lanes(c):
    out = c[:, 0:LANES]
    for j in range(1, c.shape[1] // LANES):
        out = out + c[:, j * LANES:(j + 1) * LANES]
    return out


def _attn_a_kernel(l_ref, iq_ref, ikt_ref, iw_ref, q_ref, kt_ref, v_ref, o_ref, key_scr, bias_scr,
                   *, tq, ct, n_sel):
    i = pl.program_id(1)
    pair = pl.program_id(2)
    n_chunks = i // ct + 1
    row, col = _tile_iotas(tq)
    causal = col <= row

    @pl.when(pair == 0)
    def _select():
        w_all = iw_ref[...]
        iq_heads = []
        for pr in range(IDX_Q_WIDTH // LANES):
            iq_heads += list(_head_split(iq_ref[:, pr * LANES:(pr + 1) * LANES]))

        def score_tile(kt, diag):
            ik = ikt_ref[kt]
            sc = jnp.zeros((tq, tq), F32)
            for h in range(IDX_HEADS):
                sc = sc + w_all[:, h:h + 1] * jnp.maximum(_dot(iq_heads[h], ik), 0.0)
            key = _sort_key(sc)
            if diag:
                key = jnp.where(causal, key, INT_MIN)
            key_scr[kt] = key

        def _score_body(kt, c):
            score_tile(kt, False)
            return c

        lax.fori_loop(0, i, _score_body, 0)
        score_tile(i, True)

        def count(pred):
            def body(kt, acc):
                return acc + _fold_lanes(jnp.where(pred(key_scr[kt]), 1.0, 0.0))
            acc = lax.fori_loop(0, i + 1, body, jnp.zeros((tq, LANES), F32))
            return jnp.sum(acc, axis=-1, keepdims=True)

        def search(it, cur):
            cand = cur + jnp.left_shift(jnp.int32(1), 31 - it)
            cnt = count(lambda key: key >= cand)
            return jnp.where(cnt >= n_sel, cand, cur)

        cur = lax.fori_loop(0, 32, search, jnp.full((tq, 1), INT_MIN, I32))
        thr = jnp.maximum(cur, INT_MIN + 1)
        need = n_sel - count(lambda key: key > thr)
        lmat = l_ref[...]

        def select(kt, seen):
            key = key_scr[kt]
            eq = key == thr
            eqf = jnp.where(eq, 1.0, 0.0)
            rank = _dot(eqf.astype(BF16), lmat) + seen
            sel = (key > thr) | (eq & (rank < need))
            bias_scr[kt] = jnp.where(sel, 0.0, NEG)
            return seen + jnp.sum(eqf, axis=-1, keepdims=True)

        lax.fori_loop(0, i + 1, select, jnp.zeros((tq, 1), F32))

        def mask_tile(kt, c):
            bias_scr[kt] = jnp.full((tq, tq), NEG, F32)
            return c

        lax.fori_loop(i + 1, n_chunks * ct, mask_tile, 0)

    qs = _head_split(q_ref[...])

    def chunk(c, carry):
        carry = list(carry)
        for j in range(ct):
            kt = c * ct + j
            k = kt_ref[kt]
            v = v_ref[pl.ds(pl.multiple_of(kt * tq, tq), tq), :]
            bias = bias_scr[kt]
            for h in range(2):
                carry[h] = _softmax_update(_dot(qs[h], k) + bias, *carry[h], lambda p: _dot(p, v))
        return tuple(carry)

    one = (jnp.full((tq, 1), NEG, F32), jnp.zeros((tq, 1), F32), jnp.zeros((tq, LANES), F32))
    (_, l0, a0), (_, l1, a1) = lax.fori_loop(0, n_chunks, chunk, (one, one))
    lo_half = _lane_iota() < HEAD_DIM
    o_ref[...] = jnp.where(lo_half, a0 / l0, a1 / l1).astype(o_ref.dtype)


def _strict_lower(n):
    r = lax.broadcasted_iota(I32, (n, n), 0)
    c = lax.broadcasted_iota(I32, (n, n), 1)
    return (r > c).astype(BF16)


def _prompt_mixer(kind, q, kt, v, *, extra, name):
    b, _, t, _ = q.shape
    tq = _tiles(t)["attn"]
    ct = _tiles(t)["chunk"]
    nq = t // tq
    assert nq % ct == 0
    out_shape = jax.ShapeDtypeStruct((b, t, TOK_WIDTH), BF16)
    if kind == 0:
        iq, ikt, iw, n_sel = extra
        lmat = _strict_lower(tq).T
        kern = functools.partial(_attn_a_kernel, tq=tq, ct=ct, n_sel=float(n_sel))
        return pl.pallas_call(
            kern,
            grid=(b, nq, N_PAIRS),
            in_specs=[
                pl.BlockSpec((tq, tq), lambda bi, i, p: (0, 0)),
                pl.BlockSpec((None, tq, IDX_Q_WIDTH), lambda bi, i, p: (bi, i, 0)),
                pl.BlockSpec((None, nq, LANES, tq), lambda bi, i, p: (bi, 0, 0, 0)),
                pl.BlockSpec((None, tq, LANES), lambda bi, i, p: (bi, i, 0)),
                pl.BlockSpec((None, None, tq, LANES), lambda bi, i, p: (bi, p, i, 0)),
                pl.BlockSpec((None, nq, LANES, tq), lambda bi, i, p: (bi, 0, p, 0)),
                pl.BlockSpec((None, None, t, LANES), lambda bi, i, p: (bi, p, 0, 0)),
            ],
            out_specs=pl.BlockSpec((None, tq, LANES), lambda bi, i, p: (bi, i, p)),
            out_shape=out_shape,
            scratch_shapes=[pltpu.VMEM((nq, tq, tq), I32), pltpu.VMEM((nq, tq, tq), F32)],
            compiler_params=_cparams(("parallel", "parallel", "arbitrary"), 48),
            name=name,
        )(lmat, iq, ikt, iw, q, kt, v)
    qkv_specs = [
        pl.BlockSpec((None, None, tq, LANES), lambda bi, p, i: (bi, p, i, 0)),
        pl.BlockSpec((None, nq, LANES, tq), lambda bi, p, i: (bi, 0, p, 0)),
        pl.BlockSpec((None, None, t, LANES), lambda bi, p, i: (bi, p, 0, 0)),
    ]
    out_spec = pl.BlockSpec((None, tq, LANES), lambda bi, p, i: (bi, i, p))
    if kind == 1:
        kern = functools.partial(_attn_b_kernel, tq=tq)
        consts = [_strict_lower(tq)]
        const_specs = [pl.BlockSpec((tq, tq), lambda bi, p, i: (0, 0))]
    else:
        lam_p, norm_g, lam_init = extra
        kern = functools.partial(_attn_c_kernel, tq=tq, ct=ct, lam_init=lam_init)
        consts = [lam_p, norm_g.reshape(1, LANES)]
        const_specs = [pl.BlockSpec(lam_p.shape, lambda bi, p, i: (0, 0)),
                       pl.BlockSpec((1, LANES), lambda bi, p, i: (0, 0))]
    return pl.pallas_call(
        kern,
        grid=(b, N_PAIRS, nq),
        in_specs=const_specs + qkv_specs,
        out_specs=out_spec,
        out_shape=out_shape,
        compiler_params=_cparams(("parallel", "parallel", "parallel"), 32),
        name=name,
    )(*consts, q, kt, v)


def _dec_idx_kernel(pt_ref, iq_ref, w_ref, ikn_ref, l_ref, *rest, pages, groups, n_sel):
    page_refs = rest[:pages]
    bias_ref = rest[pages]
    key_scr = rest[pages + 1]
    g = pl.program_id(1)
    ts = bias_ref.shape[1]
    width = pages * PAGE_SIZE
    iq = iq_ref[...]
    w = w_ref[...]

    def to_keys(logits):
        lg = jnp.maximum(logits, 0.0) * w
        s = lg[0:ts]
        for h in range(1, IDX_HEADS):
            s = s + lg[h * ts:(h + 1) * ts]
        return _sort_key(s)

    @pl.when(g == 0)
    def _new_keys():
        ikn = jnp.concatenate([ikn_ref[...], jnp.zeros((PAGE_SIZE - ts, IDX_DIM), F32)], axis=0)
        key = to_keys(_dot_nt(iq, ikn.astype(BF16)))
        t_idx = lax.broadcasted_iota(I32, (ts, PAGE_SIZE), 0)
        j_idx = lax.broadcasted_iota(I32, (ts, PAGE_SIZE), 1)
        key = jnp.where(j_idx <= t_idx, key, INT_MIN)
        key_scr[0] = jnp.concatenate(
            [key, jnp.full((ts, width - PAGE_SIZE), INT_MIN, I32)], axis=1)

    key_scr[g + 1] = jnp.concatenate(
        [to_keys(_dot(iq, r[...].astype(BF16))) for r in page_refs], axis=1)

    @pl.when(g == groups - 1)
    def _select():
        def count(pred):
            acc = jnp.zeros((ts, width), F32)
            for sl in range(groups + 1):
                acc = acc + jnp.where(pred(key_scr[sl]), 1.0, 0.0)
            return jnp.sum(acc, axis=-1, keepdims=True)

        def search(it, cur):
            cand = cur + jnp.left_shift(jnp.int32(1), 31 - it)
            return jnp.where(count(lambda key: key >= cand) >= n_sel, cand, cur)

        cur = lax.fori_loop(0, 32, search, jnp.full((ts, 1), INT_MIN, I32))
        thr = jnp.maximum(cur, INT_MIN + 1)
        need = n_sel - count(lambda key: key > thr)
        lmat = l_ref[...]
        seen = jnp.zeros((ts, 1), F32)
        for sl in list(range(1, groups + 1)) + [0]:
            key = key_scr[sl]
            eq = key == thr
            eqf = jnp.where(eq, 1.0, 0.0)
            eq16 = jnp.concatenate([eqf, jnp.zeros_like(eqf)], axis=0).astype(BF16)
            rank = _dot(eq16, lmat)[0:ts] + seen
            sel = (key > thr) | (eq & (rank < need))
            bias_ref[sl] = jnp.where(sel, 0.0, NEG)
            seen = seen + jnp.sum(eqf, axis=-1, keepdims=True)


def _dec_attn_kernel(pt_ref, *refs, mode, pages, groups, ts, lam_init):
    it = iter(refs)
    qbd_ref, kn_ref, vn_ref = next(it), next(it), next(it)
    bias_ref = next(it) if mode == "a" else None
    u_ref = next(it) if mode == "b" else None
    lam_ref, gn_ref = (next(it), next(it)) if mode == "c" else (None, None)
    kt_refs = [next(it) for _ in range(pages)]
    vt_refs = [next(it) for _ in range(pages)]
    o_ref = next(it)
    m_scr, l_scr, acc_scr = next(it), next(it), next(it)
    g = pl.program_id(1)
    rows = 2 * N_PAIRS * ts
    qbd = qbd_ref[...]

    def rows_of(x):
        return jnp.concatenate([x] * (2 * N_PAIRS), axis=0)

    def step(s, pvs, valid, u):
        if mode == "b":
            sp = _softplus(s)
            spm = sp if valid is None else jnp.where(valid, sp, 0.0)
            cs = _dot(spm.astype(BF16), u)
            w = jnp.exp(s - sp - (cs + m_scr[...]))
            if valid is not None:
                w = jnp.where(valid, w, 0.0)
            m_scr[...] += cs[:, 0:1] + spm[:, 0:1]
            p = w.astype(BF16)
            acc = acc_scr[...]
        else:
            if valid is not None:
                s = jnp.where(valid, s, NEG)
            m_old = m_scr[...]
            m_new = jnp.maximum(m_old, jnp.max(s, axis=-1, keepdims=True))
            a = jnp.exp(m_old - m_new)
            pf = jnp.exp(s - m_new)
            l_scr[...] = a * l_scr[...] + jnp.sum(pf, axis=-1, keepdims=True)
            m_scr[...] = m_new
            p = pf.astype(BF16)
            acc = a * acc_scr[...]
        for j, pv in enumerate(pvs):
            acc = acc + pv(p[:, j * PAGE_SIZE:(j + 1) * PAGE_SIZE])
        acc_scr[...] = acc

    @pl.when(g == 0)
    def _new_keys():
        m_scr[...] = jnp.full((rows, 1), 0.0 if mode == "b" else NEG, F32)
        l_scr[...] = jnp.zeros((rows, 1), F32)
        acc_scr[...] = jnp.zeros((rows, TOK_WIDTH), F32)
        pad = jnp.zeros((PAGE_SIZE - ts, TOK_WIDTH), F32)
        kn = jnp.concatenate([kn_ref[...], pad], axis=0).astype(BF16)
        vn = jnp.concatenate([vn_ref[...], pad], axis=0).astype(BF16)
        s = _dot_nt(qbd, kn)
        t_idx = rows_of(lax.broadcasted_iota(I32, (ts, PAGE_SIZE), 0))
        j_idx = lax.broadcasted_iota(I32, (rows, PAGE_SIZE), 1)
        valid = (j_idx < t_idx) if mode == "b" else (j_idx <= t_idx)
        if mode == "a":
            s = s + rows_of(bias_ref[:, 0:PAGE_SIZE])
        step(s, [lambda p: _dot(p, vn)], valid,
             u_ref[0:PAGE_SIZE, 0:PAGE_SIZE] if mode == "b" else None)

    @pl.when(g > 0)
    def _pages():
        s = jnp.concatenate([_dot(qbd, r[...].astype(BF16)) for r in kt_refs], axis=1)
        if mode == "a":
            s = s + rows_of(bias_ref[...])
        pvs = [functools.partial(lambda p, r: _dot_nt(p, r[...].astype(BF16)), r=r) for r in vt_refs]
        step(s, pvs, None, u_ref[...] if mode == "b" else None)

    @pl.when(g == groups)
    def _finish():
        acc = acc_scr[...]
        if mode != "b":
            acc = acc / l_scr[...]
        lo_half = _lane_iota() < HEAD_DIM
        pieces = []
        for p in range(N_PAIRS):
            blk = acc[2 * p * ts:(2 * p + 2) * ts, p * LANES:(p + 1) * LANES]
            first, second = blk[0:ts], blk[ts:2 * ts]
            if mode == "c":
                o = first - _diff_lambda(lam_ref[...], lam_init) * second
                o = o * lax.rsqrt(jnp.mean(o * o, axis=-1, keepdims=True) + RMS_EPS)
                pieces.append(o * gn_ref[...] * (1.0 - lam_init))
            else:
                pieces.append(jnp.where(lo_half, first, second))
        o_ref[...] = jnp.concatenate(pieces, axis=1)


def _pages_per_step(n_pages):
    p = 8
    while n_pages % p:
        p //= 2
    return p


def _dec_indexer(page_table, cache_ikt, slot, iq_rows, w_rows, ik_new, n_sel, name):
    bs, n_pages = page_table.shape
    ts = ik_new.shape[1]
    pages = _pages_per_step(n_pages)
    groups = n_pages // pages
    width = pages * PAGE_SIZE
    lmat = _strict_lower(width).T

    def page_spec(p):
        return pl.BlockSpec((None, None, IDX_DIM, PAGE_SIZE),
                            lambda b, g, pt: (slot, pt[b, g * pages + p], 0, 0))

    grid_spec = pltpu.PrefetchScalarGridSpec(
        num_scalar_prefetch=1,
        grid=(bs, groups),
        in_specs=[
            pl.BlockSpec((None, IDX_HEADS * ts, IDX_DIM), lambda b, g, pt: (b, 0, 0)),
            pl.BlockSpec((None, IDX_HEADS * ts, 1), lambda b, g, pt: (b, 0, 0)),
            pl.BlockSpec((None, ts, IDX_DIM), lambda b, g, pt: (b, 0, 0)),
            pl.BlockSpec((width, width), lambda b, g, pt: (0, 0)),
        ] + [page_spec(p) for p in range(pages)],
        out_specs=pl.BlockSpec((None, groups + 1, ts, width), lambda b, g, pt: (b, 0, 0, 0)),
        scratch_shapes=[pltpu.VMEM((groups + 1, ts, width), I32)],
    )
    kern = functools.partial(_dec_idx_kernel, pages=pages, groups=groups, n_sel=float(n_sel))
    return pl.pallas_call(
        kern,
        grid_spec=grid_spec,
        out_shape=jax.ShapeDtypeStruct((bs, groups + 1, ts, width), F32),
        compiler_params=_cparams(("parallel", "arbitrary"), 32),
        name=name,
    )(page_table, iq_rows, w_rows, ik_new, lmat, *([cache_ikt] * pages))


def _dec_mixer(mode, layer, page_table, cache_kt, cache_vt, qbd, k_new, v_new, *, extra, name):
    bs, n_pages = page_table.shape
    ts = k_new.shape[1]
    pages = _pages_per_step(n_pages)
    groups = n_pages // pages
    width = pages * PAGE_SIZE
    rows = 2 * N_PAIRS * ts
    descending = mode == "b"

    def page_spec(p):
        def imap(b, g, pt):
            grp = jnp.maximum(g - 1, 0)
            if descending:
                grp = groups - 1 - grp
            return (layer, pt[b, grp * pages + p], 0, 0)
        return pl.BlockSpec((None, None, TOK_WIDTH, PAGE_SIZE), imap)

    in_specs = [
        pl.BlockSpec((None, rows, TOK_WIDTH), lambda b, g, pt: (b, 0, 0)),
        pl.BlockSpec((None, ts, TOK_WIDTH), lambda b, g, pt: (b, 0, 0)),
        pl.BlockSpec((None, ts, TOK_WIDTH), lambda b, g, pt: (b, 0, 0)),
    ]
    args = [qbd, k_new, v_new]
    lam_init = 0.0
    if mode == "a":
        in_specs.append(pl.BlockSpec((None, None, ts, width), lambda b, g, pt: (b, g, 0, 0)))
        args.append(extra)
    elif mode == "b":
        in_specs.append(pl.BlockSpec((width, width), lambda b, g, pt: (0, 0)))
        args.append(_strict_lower(width))
    else:
        lam_p, norm_g, lam_init = extra
        in_specs += [pl.BlockSpec(lam_p.shape, lambda b, g, pt: (0, 0)),
                     pl.BlockSpec((1, LANES), lambda b, g, pt: (0, 0))]
        args += [lam_p, norm_g.reshape(1, LANES)]
    in_specs += [page_spec(p) for p in range(pages)] * 2
    args += [cache_kt] * pages + [cache_vt] * pages
    grid_spec = pltpu.PrefetchScalarGridSpec(
        num_scalar_prefetch=1,
        grid=(bs, groups + 1),
        in_specs=in_specs,
        out_specs=pl.BlockSpec((None, ts, TOK_WIDTH), lambda b, g, pt: (b, 0, 0)),
        scratch_shapes=[pltpu.VMEM((rows, 1), F32), pltpu.VMEM((rows, 1), F32),
                        pltpu.VMEM((rows, TOK_WIDTH), F32)],
    )
    kern = functools.partial(_dec_attn_kernel, mode=mode, pages=pages, groups=groups, ts=ts,
                             lam_init=lam_init)
    return pl.pallas_call(
        kern,
        grid_spec=grid_spec,
        out_shape=jax.ShapeDtypeStruct((bs, ts, TOK_WIDTH), F32),
        compiler_params=_cparams(("parallel", "arbitrary"), 48),
        name=name,
    )(page_table, *args)


def _oproj_ln_kernel(tok_ref, mem_ref, x_ref, wt_ref, wm_ref, g_ref, b_ref, o_ref, obf_ref, *, alpha):
    mix = (_dot(tok_ref[...].astype(BF16), wt_ref[...])
           + _dot(mem_ref[...].astype(BF16), wm_ref[...]))
    out = _layer_norm(alpha * x_ref[...] + mix, g_ref[...], b_ref[...])
    o_ref[...] = out
    obf_ref[...] = out.astype(BF16)


def _oproj_ln(tok, mem, x, w_tok, w_mem, g, b, alpha, name):
    m_rows, d = x.shape
    tm = _tiles(m_rows)["proj"]
    row = lambda n: pl.BlockSpec((tm, n), lambda i: (i, 0))
    full = lambda a: pl.BlockSpec(a.shape, lambda i: (0, 0))
    return pl.pallas_call(
        functools.partial(_oproj_ln_kernel, alpha=alpha),
        grid=(m_rows // tm,),
        in_specs=[row(TOK_WIDTH), row(MEM_WIDTH), row(d), full(w_tok), full(w_mem), full(g), full(b)],
        out_specs=[row(d), row(d)],
        out_shape=[jax.ShapeDtypeStruct((m_rows, d), F32), jax.ShapeDtypeStruct((m_rows, d), BF16)],
        compiler_params=_cparams(("parallel",), 40),
        name=name,
    )(tok, mem, x, w_tok, w_mem, g, b)


def _ffn_ln_kernel(xbf_ref, x_ref, wg_ref, wu_ref, wd_ref, g_ref, b_ref, o_ref, obf_ref, acc_ref,
                   *, alpha, n_f):
    f = pl.program_id(1)
    xb = xbf_ref[...]
    hg = _dot(xb, wg_ref[...])
    hu = _dot(xb, wu_ref[...])
    act = hg * (1.0 / (1.0 + jnp.exp(-hg))) * hu
    part = _dot(act.astype(BF16), wd_ref[...])

    @pl.when(f == 0)
    def _first():
        acc_ref[...] = part

    @pl.when(f > 0)
    def _rest():
        acc_ref[...] += part

    @pl.when(f == n_f - 1)
    def _finish():
        out = _layer_norm(alpha * x_ref[...] + acc_ref[...], g_ref[...], b_ref[...])
        o_ref[...] = out
        obf_ref[...] = out.astype(BF16)


def _ffn_ln(xbf, x, w_gu, w_d, g, b, alpha, name):
    m_rows, d = x.shape
    d_ff = w_d.shape[0]
    tm = _tiles(m_rows)["ffn"]
    n_f = 2
    tf = d_ff // n_f
    row = lambda: pl.BlockSpec((tm, d), lambda i, f: (i, 0))
    vec = lambda a: pl.BlockSpec(a.shape, lambda i, f: (0, 0))
    return pl.pallas_call(
        functools.partial(_ffn_ln_kernel, alpha=alpha, n_f=n_f),
        grid=(m_rows // tm, n_f),
        in_specs=[row(), row(),
                  pl.BlockSpec((d, tf), lambda i, f: (0, f)),
                  pl.BlockSpec((d, tf), lambda i, f: (0, f + n_f)),
                  pl.BlockSpec((tf, d), lambda i, f: (f, 0)),
                  vec(g), vec(b)],
        out_specs=[row(), row()],
        out_shape=[jax.ShapeDtypeStruct((m_rows, d), F32), jax.ShapeDtypeStruct((m_rows, d), BF16)],
        scratch_shapes=[pltpu.VMEM((tm, d), F32)],
        compiler_params=_cparams(("parallel", "arbitrary"), 56),
        name=name,
    )(xbf, x, w_gu, w_gu, w_d, g, b)


def _rope_tables(pos):
    half = HEAD_DIM // 2
    inv = ROPE_THETA ** (-jnp.arange(half, dtype=F32) / half)
    ang = pos.astype(F32)[:, None] * inv[None, :]
    cos = jnp.cos(ang)
    sin = jnp.sin(ang)
    return jnp.tile(cos, (1, 4)), jnp.tile(jnp.concatenate([-sin, sin], axis=1), (1, 2))


def _heads_last(xt, n_heads):
    l, b, _, p = xt.shape
    return xt.reshape(l, b, n_heads, HEAD_DIM, p).transpose(0, 1, 4, 2, 3)


def kernel(x_prompt, x_sample, mem_prompt, cache_k, cache_v, cache_idx_k, cache_mem_k, cache_mem_v, page_table, w_in, w_idx, w_mem_kv, w_o, ln_mix_g, ln_mix_b, w_gate_up, w_down, ln_ffn_g, ln_ffn_b, diff_lambda, diff_norm_g):
    bp, tp, d_model = x_prompt.shape
    bs, ts, _ = x_sample.shape
    n_pages = page_table.shape[1]
    past = n_pages * PAGE_SIZE
    depth = w_in.shape[0]
    n_mem = mem_prompt.shape[1]
    n_pool = cache_k.shape[1]
    n_heads = 2 * N_PAIRS
    mem_heads = MEM_WIDTH // HEAD_DIM
    alpha = (2 * depth) ** 0.25
    nsel_p = min(TOPK_MAX, tp // 4)
    nsel_s = min(TOPK_MAX, (past + ts) // 4)
    mp, ms = bp * tp, bs * ts

    w_in_b = w_in.astype(BF16)
    w_mkv_b = w_mem_kv.astype(BF16)
    w_o_b = w_o.astype(BF16)
    w_gu_b = w_gate_up.astype(BF16)
    w_d_b = w_down.astype(BF16)
    iq_w, ik_w, ih_w = (w_idx[..., :IDX_Q_WIDTH], w_idx[..., IDX_Q_WIDTH:IDX_Q_WIDTH + IDX_DIM],
                        w_idx[..., IDX_Q_WIDTH + IDX_DIM:])
    w_idx_b = jnp.concatenate(
        [iq_w, ik_w, ik_w, ih_w, jnp.zeros(ih_w.shape[:2] + (LANES - IDX_HEADS,), F32)],
        axis=-1).astype(BF16)
    idx_cols = IDX_Q_WIDTH + 2 * LANES

    rope_p = _rope_tables(jnp.arange(tp))
    rope_s = _rope_tables(jnp.tile(past + jnp.arange(ts), bs))
    cache_kt = cache_k.transpose(0, 1, 3, 4, 2).reshape(depth, n_pool, TOK_WIDTH, PAGE_SIZE)
    cache_vt = cache_v.transpose(0, 1, 3, 4, 2).reshape(depth, n_pool, TOK_WIDTH, PAGE_SIZE)
    cache_ikt = cache_idx_k.transpose(0, 1, 3, 2)
    cache_mkt = cache_mem_k.transpose(0, 1, 3, 4, 2).reshape(depth, bs, MEM_WIDTH, n_mem)
    cache_mvt = cache_mem_v.transpose(0, 1, 3, 4, 2).reshape(depth, bs, MEM_WIDTH, n_mem)
    mem2d = mem_prompt.reshape(bp * n_mem, d_model)
    head_of_col = jnp.arange(TOK_WIDTH) // HEAD_DIM
    bd_mask = (jnp.arange(n_heads)[:, None, None] == head_of_col[None, None, :])

    xp = x_prompt.reshape(mp, d_model)
    xs = x_sample.reshape(ms, d_model)
    xp_in, xs_in = xp, xs
    nidx_p, nmk_p, nmv_p, nk_s, nv_s, nidx_s = [], [], [], [], [], []
    kv_stack = None

    for l in range(depth):
        kind = l % N_MIXERS
        slot = l // N_MIXERS
        n_rope = n_heads if kind != 1 else 0
        tag = f"l{l}"
        c_q, c_k, c_v, c_m = 0, TOK_WIDTH, 2 * TOK_WIDTH, 3 * TOK_WIDTH
        in_w = c_m + MEM_WIDTH

        q_p, kt_f, kt_p, vt_f, v_p, mq_p = _proj(
            xp_in, w_in_b[l],
            (("pair", c_q, c_k, QK_SCALE, BF16), ("Tstack", c_k, c_v, 1.0, F32),
             ("Ttile", c_k, c_v, 1.0, BF16), ("Tstack", c_v, c_m, 1.0, F32),
             ("pair", c_v, c_m, 1.0, BF16), ("flat", c_m, in_w, QK_SCALE, BF16)),
            batch=bp, rope=rope_p, n_rope_chunks=n_rope, stack=(depth, l, kv_stack),
            name=f"proj_p_{tag}")
        kv_stack = (kt_f, vt_f)
        mkt_f, mvt_f = _proj(
            mem2d, w_mkv_b[l],
            (("T", 0, MEM_WIDTH, 1.0, F32), ("T", MEM_WIDTH, 2 * MEM_WIDTH, 1.0, F32)),
            batch=bp, name=f"proj_mem_{tag}")
        nmk_p.append(mkt_f)
        nmv_p.append(mvt_f)
        mem_p = _mem_attn(mq_p.reshape(bp, tp, MEM_WIDTH), mkt_f, mvt_f, None, BF16, f"mem_p_{tag}")

        q_s, k_s, v_s, mq_s = _proj(
            xs_in, w_in_b[l],
            (("flat", c_q, c_k, QK_SCALE, F32), ("flat", c_k, c_v, 1.0, F32),
             ("flat", c_v, c_m, 1.0, F32), ("flat", c_m, in_w, QK_SCALE, F32)),
            batch=bs, rope=rope_s, n_rope_chunks=n_rope, name=f"proj_s_{tag}")
        k_s3 = k_s.reshape(bs, ts, TOK_WIDTH)
        v_s3 = v_s.reshape(bs, ts, TOK_WIDTH)
        nk_s.append(k_s3.reshape(bs, ts, n_heads, HEAD_DIM))
        nv_s.append(v_s3.reshape(bs, ts, n_heads, HEAD_DIM))
        qbd = jnp.where(bd_mask[None], q_s.reshape(bs, 1, ts, TOK_WIDTH), 0.0)
        qbd = qbd.reshape(bs, n_heads * ts, TOK_WIDTH).astype(BF16)
        mem_s = _mem_attn(mq_s.reshape(bs, ts, MEM_WIDTH), cache_mkt, cache_mvt, l, F32,
                          f"mem_s_{tag}")

        if kind == 0:
            iq_p, ikt_p, ikt_f, iw_p = _proj(
                xp_in, w_idx_b[slot],
                (("flat", 0, IDX_Q_WIDTH, 1.0, BF16),
                 ("Ttile", IDX_Q_WIDTH, IDX_Q_WIDTH + LANES, 1.0, BF16),
                 ("T", IDX_Q_WIDTH, IDX_Q_WIDTH + IDX_DIM, 1.0, F32),
                 ("flat", IDX_Q_WIDTH + LANES, idx_cols, IDX_W_SCALE, F32)),
                batch=bp, rope=rope_p, n_rope_chunks=IDX_Q_WIDTH // LANES + 1, name=f"proj_ip_{tag}")
            nidx_p.append(ikt_f)
            tok_p = _prompt_mixer(
                0, q_p, kt_p, v_p,
                extra=(iq_p.reshape(bp, tp, IDX_Q_WIDTH), ikt_p, iw_p.reshape(bp, tp, LANES), nsel_p),
                name=f"mix_a_{tag}")

            iq_s, ik_s, iw_s = _proj(
                xs_in, w_idx_b[slot],
                (("flat", 0, IDX_Q_WIDTH, 1.0, F32),
                 ("flat", IDX_Q_WIDTH, IDX_Q_WIDTH + IDX_DIM, 1.0, F32),
                 ("flat", IDX_Q_WIDTH + LANES, idx_cols, IDX_W_SCALE, F32)),
                batch=bs, rope=rope_s, n_rope_chunks=IDX_Q_WIDTH // LANES + 1, name=f"proj_is_{tag}")
            ik_s3 = ik_s.reshape(bs, ts, IDX_DIM)
            nidx_s.append(ik_s3)
            iq_rows = iq_s.reshape(bs, ts, IDX_HEADS, IDX_DIM).swapaxes(1, 2)
            iq_rows = iq_rows.reshape(bs, IDX_HEADS * ts, IDX_DIM).astype(BF16)
            w_rows = iw_s.reshape(bs, ts, LANES)[:, :, :IDX_HEADS].swapaxes(1, 2)
            w_rows = w_rows.reshape(bs, IDX_HEADS * ts, 1)
            bias = _dec_indexer(page_table, cache_ikt, slot, iq_rows, w_rows, ik_s3, nsel_s,
                                f"idx_s_{tag}")
            tok_s = _dec_mixer("a", l, page_table, cache_kt, cache_vt, qbd, k_s3, v_s3,
                               extra=bias, name=f"mix_sa_{tag}")
        elif kind == 1:
            tok_p = _prompt_mixer(1, q_p, kt_p, v_p, extra=None, name=f"mix_b_{tag}")
            tok_s = _dec_mixer("b", l, page_table, cache_kt, cache_vt, qbd, k_s3, v_s3,
                               extra=None, name=f"mix_sb_{tag}")
        else:
            lam_init = 0.8 - 0.6 * math.exp(-0.3 * l)
            extra = (diff_lambda[slot], diff_norm_g[slot], lam_init)
            tok_p = _prompt_mixer(2, q_p, kt_p, v_p, extra=extra, name=f"mix_c_{tag}")
            tok_s = _dec_mixer("c", l, page_table, cache_kt, cache_vt, qbd, k_s3, v_s3,
                               extra=extra, name=f"mix_sc_{tag}")

        w_tok, w_mem = w_o_b[l, :TOK_WIDTH], w_o_b[l, TOK_WIDTH:]
        g1, b1 = ln_mix_g[l].reshape(1, d_model), ln_mix_b[l].reshape(1, d_model)
        g2, b2 = ln_ffn_g[l].reshape(1, d_model), ln_ffn_b[l].reshape(1, d_model)
        x1, x1b = _oproj_ln(tok_p.reshape(mp, TOK_WIDTH), mem_p.reshape(mp, MEM_WIDTH), xp,
                            w_tok, w_mem, g1, b1, alpha, f"oproj_p_{tag}")
        xp, xp_in = _ffn_ln(x1b, x1, w_gu_b[l], w_d_b[l], g2, b2, alpha, f"ffn_p_{tag}")
        y1, y1b = _oproj_ln(tok_s.reshape(ms, TOK_WIDTH), mem_s.reshape(ms, MEM_WIDTH), xs,
                            w_tok, w_mem, g1, b1, alpha, f"oproj_s_{tag}")
        xs, xs_in = _ffn_ln(y1b, y1, w_gu_b[l], w_d_b[l], g2, b2, alpha, f"ffn_s_{tag}")

    return (xp.reshape(bp, tp, d_model), xs.reshape(bs, ts, d_model),
            _heads_last(kv_stack[0], n_heads), _heads_last(kv_stack[1], n_heads),
            jnp.stack(nidx_p).transpose(0, 1, 3, 2),
            _heads_last(jnp.stack(nmk_p), mem_heads), _heads_last(jnp.stack(nmv_p), mem_heads),
            jnp.stack(nk_s), jnp.stack(nv_s), jnp.stack(nidx_s))
```

```python
import functools
import math

import jax
import jax.numpy as jnp
from jax import lax
from jax.experimental import pallas as pl
from jax.experimental.pallas import tpu as pltpu

F32 = jnp.float32
BF16 = jnp.bfloat16
I32 = jnp.int32

HEAD_DIM = 64
TOK_WIDTH = 768
N_PAIRS = 6
MEM_WIDTH = 256
IDX_HEADS = 8
IDX_DIM = 64
IDX_Q_WIDTH = IDX_HEADS * IDX_DIM
IDX_W_SCALE = IDX_HEADS ** -0.5 * IDX_DIM ** -0.5
PAGE_SIZE = 128
TOPK_MAX = 256
N_MIXERS = 3
ROPE_THETA = 10000.0
LN_EPS = 1e-5
RMS_EPS = 1e-5
QK_SCALE = HEAD_DIM ** -0.5

LANES = 128
NEG = -1e30
STICK_EXIT = 110.0
INT_MIN = -2 ** 31
MIB = 1024 * 1024


def _tiles(rows):
    attn = min(256, rows)
    return dict(proj=min(512, rows), ffn=min(512, rows), attn=attn, chunk=min(4, rows // attn))


def _cparams(sem, vmem_mib):
    return pltpu.CompilerParams(dimension_semantics=sem, vmem_limit_bytes=vmem_mib * MIB)


def _dot(a, b):
    return jnp.dot(a, b, preferred_element_type=F32)


def _dot_nt(a, b):
    return lax.dot_general(a, b, (((1,), (1,)), ((), ())), preferred_element_type=F32)


def _lane_iota():
    return lax.broadcasted_iota(I32, (1, LANES), 1)


def _layer_norm(y, g, b):
    mu = jnp.mean(y, axis=-1, keepdims=True)
    d = y - mu
    var = jnp.mean(d * d, axis=-1, keepdims=True)
    return d * lax.rsqrt(var + LN_EPS) * g + b


def _softmax_update(s, m, l, acc, pv):
    m_new = jnp.maximum(m, jnp.max(s, axis=-1, keepdims=True))
    a = jnp.exp(m - m_new)
    p = jnp.exp(s - m_new)
    l = a * l + jnp.sum(p, axis=-1, keepdims=True)
    acc = a * acc + pv(p.astype(BF16))
    return m_new, l, acc


def _sort_key(x):
    bits = lax.bitcast_convert_type(x, I32)
    return bits ^ ((bits >> 31) & 0x7FFFFFFF)


def _softplus(z):
    return jnp.maximum(z, 0.0) + jnp.log(1.0 + jnp.exp(-jnp.abs(z)))


def _diff_lambda(lp, lam_init):
    return (jnp.exp(jnp.sum(lp[0:1] * lp[1:2], axis=-1, keepdims=True))
            - jnp.exp(jnp.sum(lp[2:3] * lp[3:4], axis=-1, keepdims=True)) + lam_init)


def _head_split(q):
    lo_half = _lane_iota() < HEAD_DIM
    zero = jnp.zeros_like(q)
    return jnp.where(lo_half, q, zero), jnp.where(lo_half, zero, q)


def _proj_kernel(*refs, n_rope_chunks, outs, tk):
    x_ref, w_ref = refs[:2]
    if n_rope_chunks:
        cos_ref, sin_ref = refs[2:4]
    o_refs = refs[len(refs) - len(outs):]
    y = _dot(x_ref[...].astype(BF16), w_ref[...])
    n_chunks = y.shape[1] // LANES
    chunks = [y[:, c * LANES:(c + 1) * LANES] for c in range(n_chunks)]
    if n_rope_chunks:
        cos = cos_ref[...]
        sin = sin_ref[...]
        first = (_lane_iota() & (HEAD_DIM - 1)) < HEAD_DIM // 2
        for c in range(n_rope_chunks):
            yc = chunks[c]
            partner = jnp.where(first, pltpu.roll(yc, LANES - HEAD_DIM // 2, 1),
                                pltpu.roll(yc, HEAD_DIM // 2, 1))
            chunks[c] = yc * cos + partner * sin
    for o_ref, (kind, lo, hi, scale) in zip(o_refs, outs):
        c0, c1 = lo // LANES, -(-hi // LANES)
        if kind == "pair":
            for p in range(c1 - c0):
                v = chunks[c0 + p]
                o_ref[p] = (v * scale if scale != 1.0 else v).astype(o_ref.dtype)
            continue
        v = chunks[c0] if c1 - c0 == 1 else jnp.concatenate(chunks[c0:c1], axis=1)
        if scale != 1.0:
            v = v * scale
        if kind == "flat":
            o_ref[...] = v[:, lo - c0 * LANES:hi - c0 * LANES].astype(o_ref.dtype)
        elif kind in ("T", "Tstack"):
            o_ref[...] = v.T[0:hi - lo].astype(o_ref.dtype)
        else:
            vt = v.T.astype(o_ref.dtype)
            for j in range(vt.shape[1] // tk):
                o_ref[j] = vt[:, j * tk:(j + 1) * tk]


def _proj(x, w, outs, *, batch, rope=None, n_rope_chunks=0, stack=None, name):
    m_rows, k_dim = x.shape
    n_cols = w.shape[1]
    t_len = m_rows // batch
    structured = any(o[0] != "flat" for o in outs)
    tm = _tiles(t_len if structured else m_rows)["proj"]
    tk = _tiles(t_len)["attn"]
    tiles_per_seq = max(t_len // tm, 1)
    in_specs = [pl.BlockSpec((tm, k_dim), lambda i: (i, 0)),
                pl.BlockSpec((k_dim, n_cols), lambda i: (0, 0))]
    args = [x, w]
    if n_rope_chunks:
        cos, sin = rope
        n_tab = cos.shape[0] // tm
        in_specs += [pl.BlockSpec((tm, LANES), lambda i: (i % n_tab, 0))] * 2
        args += [cos, sin]
    seq_of = lambda i: i // tiles_per_seq
    tile_of = lambda i: i % tiles_per_seq
    out_shapes, out_specs, aliases = [], [], {}
    for o_idx, (kind, lo, hi, _, dtype) in enumerate(outs):
        if kind == "Tstack":
            depth, layer, prev = stack
            shape, block = (depth, batch, hi - lo, t_len), (None, None, hi - lo, tm)
            imap = lambda i: (layer, seq_of(i), 0, tile_of(i))
            if prev is not None:
                aliases[len(args)] = o_idx
                in_specs.append(pl.BlockSpec(memory_space=pl.ANY))
                args.append(prev[sum(o[0] == "Tstack" for o in outs[:o_idx])])
        elif kind == "pair":
            groups = (hi - lo) // LANES
            shape, block = (batch, groups, t_len, LANES), (None, groups, tm, LANES)
            imap = lambda i: (seq_of(i), 0, tile_of(i), 0)
        elif kind == "T":
            shape, block = (batch, hi - lo, t_len), (None, hi - lo, tm)
            imap = lambda i: (seq_of(i), 0, tile_of(i))
        elif kind == "Ttile":
            shape, block = (batch, t_len // tk, hi - lo, tk), (None, tm // tk, hi - lo, tk)
            imap = lambda i: (seq_of(i), tile_of(i), 0, 0)
        else:
            shape, block = (m_rows, hi - lo), (tm, hi - lo)
            imap = lambda i: (i, 0)
        out_shapes.append(jax.ShapeDtypeStruct(shape, dtype))
        out_specs.append(pl.BlockSpec(block, imap))
    kern = functools.partial(_proj_kernel, n_rope_chunks=n_rope_chunks,
                             outs=tuple(o[:4] for o in outs), tk=tk)
    return pl.pallas_call(
        kern,
        grid=(m_rows // tm,),
        in_specs=in_specs,
        out_specs=out_specs,
        out_shape=out_shapes,
        input_output_aliases=aliases,
        compiler_params=_cparams(("parallel",), 56),
        name=name,
    )(*args)


def _mem_attn_kernel(q_ref, kt_ref, vt_ref, o_ref):
    lo_half = _lane_iota() < HEAD_DIM
    pieces = []
    for pr in range(MEM_WIDTH // LANES):
        sl = slice(pr * LANES, (pr + 1) * LANES)
        kt = kt_ref[sl, :].astype(BF16)
        vt = vt_ref[sl, :].astype(BF16)
        res = []
        for qc in _head_split(q_ref[:, sl].astype(BF16)):
            s = _dot(qc, kt)
            m = jnp.max(s, axis=-1, keepdims=True)
            p = jnp.exp(s - m)
            l = jnp.sum(p, axis=-1, keepdims=True)
            res.append(_dot_nt(p.astype(BF16), vt) / l)
        pieces.append(jnp.where(lo_half, res[0], res[1]))
    o_ref[...] = jnp.concatenate(pieces, axis=1).astype(o_ref.dtype)


def _mem_attn(q, kt, vt, layer, out_dtype, name):
    b, t, _ = q.shape
    tq = _tiles(t)["attn"]
    n_mem = kt.shape[-1]
    if layer is None:
        kv_spec = pl.BlockSpec((None, MEM_WIDTH, n_mem), lambda bi, i: (bi, 0, 0))
    else:
        kv_spec = pl.BlockSpec((None, None, MEM_WIDTH, n_mem), lambda bi, i: (layer, bi, 0, 0))
    return pl.pallas_call(
        _mem_attn_kernel,
        grid=(b, t // tq),
        in_specs=[pl.BlockSpec((None, tq, MEM_WIDTH), lambda bi, i: (bi, i, 0)), kv_spec, kv_spec],
        out_specs=pl.BlockSpec((None, tq, MEM_WIDTH), lambda bi, i: (bi, i, 0)),
        out_shape=jax.ShapeDtypeStruct((b, t, MEM_WIDTH), out_dtype),
        compiler_params=_cparams(("parallel", "parallel"), 32),
        name=name,
    )(q, kt, vt)


def _tile_iotas(t):
    return (lax.broadcasted_iota(I32, (t, t), 0), lax.broadcasted_iota(I32, (t, t), 1))


def _attn_c_kernel(lam_ref, g_ref, q_ref, kt_ref, v_ref, o_ref, *, tq, ct, lam_init):
    i = pl.program_id(2)
    diag_chunk = i // ct
    qs = _head_split(q_ref[...])

    row, col = _tile_iotas(tq)

    def chunk(c, carry, diag):
        carry = list(carry)
        for j in range(ct):
            kt = c * ct + j
            k = kt_ref[kt]
            v = v_ref[pl.ds(pl.multiple_of(kt * tq, tq), tq), :]
            for h in range(2):
                s = _dot(qs[h], k)
                if diag:
                    s = jnp.where(col + kt * tq <= row + i * tq, s, NEG)
                carry[h] = _softmax_update(s, *carry[h], lambda p: _dot(p, v))
        return tuple(carry)

    one = (jnp.full((tq, 1), NEG, F32), jnp.zeros((tq, 1), F32), jnp.zeros((tq, LANES), F32))
    carry = lax.fori_loop(0, diag_chunk, lambda c, cr: chunk(c, cr, False), (one, one))
    (_, l0, a0), (_, l1, a1) = chunk(diag_chunk, carry, True)
    o = a0 / l0 - _diff_lambda(lam_ref[...], lam_init) * (a1 / l1)
    o = o * lax.rsqrt(jnp.mean(o * o, axis=-1, keepdims=True) + RMS_EPS)
    o_ref[...] = (o * g_ref[...] * (1.0 - lam_init)).astype(o_ref.dtype)


def _attn_b_kernel(u_ref, q_ref, kt_ref, v_ref, o_ref, *, tq):
    i = pl.program_id(2)
    qs = _head_split(q_ref[...])
    u = u_ref[...]
    row, col = _tile_iotas(tq)
    strict = col < row

    def tile(kt, carry, diag):
        k = kt_ref[kt]
        v = v_ref[pl.ds(pl.multiple_of(kt * tq, tq), tq), :]
        new = []
        for c in range(2):
            later, acc = carry[c]
            z = _dot(qs[c], k)
            sp = _softplus(z)
            spm = jnp.where(strict, sp, 0.0) if diag else sp
            cs = _dot(spm.astype(BF16), u)
            w = jnp.exp(z - sp - (cs + later))
            if diag:
                w = jnp.where(strict, w, 0.0)
            acc = acc + _dot(w.astype(BF16), v)
            later = later + cs[:, 0:1] + spm[:, 0:1]
            new.append((later, acc))
        return tuple(new)

    def live(carry):
        return (jnp.minimum(jnp.min(carry[0][0]), jnp.min(carry[1][0])) < STICK_EXIT).astype(I32)

    def cond(state):
        return (state[0] >= 0) & (state[1] > 0)

    def body(state):
        carry = tile(state[0], state[2], False)
        return state[0] - 1, live(carry), carry

    one = (jnp.zeros((tq, 1), F32), jnp.zeros((tq, LANES), F32))
    carry = tile(i, (one, one), True)
    carry = lax.while_loop(cond, body, (i - 1, live(carry), carry))[2]
    lo_half = _lane_iota() < HEAD_DIM
    o_ref[...] = jnp.where(lo_half, carry[0][1], carry[1][1]).astype(o_ref.dtype)


def _fold_lanes(c):
    out = c[:, 0:LANES]
    for j in range(1, c.shape[1] // LANES):
        out = out + c[:, j * LANES:(j + 1) * LANES]
    return out


def _attn_a_kernel(l_ref, iq_ref, ikt_ref, iw_ref, q_ref, kt_ref, v_ref, o_ref, key_scr, bias_scr,
                   *, tq, ct, n_sel):
    i = pl.program_id(1)
    pair = pl.program_id(2)
    n_chunks = i // ct + 1
    row, col = _tile_iotas(tq)
    causal = col <= row

    @pl.when(pair == 0)
    def _select():
        w_all = iw_ref[...]
        iq_heads = []
        for pr in range(IDX_Q_WIDTH // LANES):
            iq_heads += list(_head_split(iq_ref[:, pr * LANES:(pr + 1) * LANES]))

        def score_tile(kt, diag):
            ik = ikt_ref[kt]
            sc = jnp.zeros((tq, tq), F32)
            for h in range(IDX_HEADS):
                sc = sc + w_all[:, h:h + 1] * jnp.maximum(_dot(iq_heads[h], ik), 0.0)
            key = _sort_key(sc)
            if diag:
                key = jnp.where(causal, key, INT_MIN)
            key_scr[kt] = key

        def _score_body(kt, c):
            score_tile(kt, False)
            return c

        lax.fori_loop(0, i, _score_body, 0)
        score_tile(i, True)

        def _pad_body(kt, c):
            bias_scr[kt] = jnp.full((tq, tq), NEG, F32)
            return c

        lax.fori_loop(i + 1, n_chunks * ct, _pad_body, 0)

        def count(pred):
            def body(kt, acc):
                return acc + _fold_lanes(jnp.where(pred(key_scr[kt]), 1.0, 0.0))
            acc = lax.fori_loop(0, i + 1, body, jnp.zeros((tq, LANES), F32))
            return jnp.sum(acc, axis=-1, keepdims=True)

        def search(it, cur):
            cand = cur + jnp.left_shift(jnp.int32(1), 31 - it)
            cnt = count(lambda key: key >= cand)
            return jnp.where(cnt >= n_sel, cand, cur)

        cur = lax.fori_loop(0, 32, search, jnp.full((tq, 1), INT_MIN, I32))
        thr = jnp.maximum(cur, INT_MIN + 1)
        ties = jnp.max(count(lambda key: key >= thr)) > n_sel

        @pl.when(jnp.logical_not(ties))
        def _by_threshold():
            def body(kt, c):
                bias_scr[kt] = jnp.where(key_scr[kt] >= thr, 0.0, NEG)
                return c
            lax.fori_loop(0, i + 1, body, 0)

        @pl.when(ties)
        def _by_rank():
            need = n_sel - count(lambda key: key > thr)
            lmat = l_ref[...]

            def body(kt, seen):
                key = key_scr[kt]
                eq = key == thr
                eqf = jnp.where(eq, 1.0, 0.0)
                rank = _dot(eqf.astype(BF16), lmat) + seen
                sel = (key > thr) | (eq & (rank < need))
                bias_scr[kt] = jnp.where(sel, 0.0, NEG)
                return seen + jnp.sum(eqf, axis=-1, keepdims=True)

            lax.fori_loop(0, i + 1, body, jnp.zeros((tq, 1), F32))

    qs = _head_split(q_ref[...])
    lo_half = _lane_iota() < HEAD_DIM
    one_bf = jnp.ones((tq, LANES), BF16)

    def chunk(c, carry):
        carry = list(carry)
        for j in range(ct):
            kt = c * ct + j
            k = kt_ref[kt]
            v = v_ref[pl.ds(pl.multiple_of(kt * tq, tq), tq), :]
            vs = (jnp.where(lo_half, v, one_bf), jnp.where(lo_half, one_bf, v))
            bias = bias_scr[kt]
            for h in range(2):
                m, acc = carry[h]
                s = _dot(qs[h], k) + bias
                m_new = jnp.maximum(m, jnp.max(s, axis=-1, keepdims=True))
                p = jnp.exp(s - m_new).astype(BF16)
                carry[h] = (m_new, jnp.exp(m - m_new) * acc + _dot(p, vs[h]))
        return tuple(carry)

    one = (jnp.full((tq, 1), NEG, F32), jnp.zeros((tq, LANES), F32))
    (_, a0), (_, a1) = lax.fori_loop(0, n_chunks, chunk, (one, one))
    half = HEAD_DIM
    o_ref[...] = jnp.where(lo_half, a0 / pltpu.roll(a0, half, 1),
                           a1 / pltpu.roll(a1, half, 1)).astype(o_ref.dtype)


def _strict_lower(n):
    r = lax.broadcasted_iota(I32, (n, n), 0)
    c = lax.broadcasted_iota(I32, (n, n), 1)
    return (r > c).astype(BF16)


def _prompt_mixer(kind, q, kt, v, *, extra, name):
    b, _, t, _ = q.shape
    tq = _tiles(t)["attn"]
    ct = _tiles(t)["chunk"]
    nq = t // tq
    assert nq % ct == 0
    out_shape = jax.ShapeDtypeStruct((b, t, TOK_WIDTH), BF16)
    if kind == 0:
        iq, ikt, iw, n_sel = extra
        lmat = _strict_lower(tq).T
        kern = functools.partial(_attn_a_kernel, tq=tq, ct=ct, n_sel=float(n_sel))
        return pl.pallas_call(
            kern,
            grid=(b, nq, N_PAIRS),
            in_specs=[
                pl.BlockSpec((tq, tq), lambda bi, i, p: (0, 0)),
                pl.BlockSpec((None, tq, IDX_Q_WIDTH), lambda bi, i, p: (bi, i, 0)),
                pl.BlockSpec((None, nq, LANES, tq), lambda bi, i, p: (bi, 0, 0, 0)),
                pl.BlockSpec((None, tq, LANES), lambda bi, i, p: (bi, i, 0)),
                pl.BlockSpec((None, None, tq, LANES), lambda bi, i, p: (bi, p, i, 0)),
                pl.BlockSpec((None, nq, LANES, tq), lambda bi, i, p: (bi, 0, p, 0)),
                pl.BlockSpec((None, None, t, LANES), lambda bi, i, p: (bi, p, 0, 0)),
            ],
            out_specs=pl.BlockSpec((None, tq, LANES), lambda bi, i, p: (bi, i, p)),
            out_shape=out_shape,
            scratch_shapes=[pltpu.VMEM((nq, tq, tq), I32), pltpu.VMEM((nq, tq, tq), F32)],
            compiler_params=_cparams(("parallel", "parallel", "arbitrary"), 48),
            name=name,
        )(lmat, iq, ikt, iw, q, kt, v)
    qkv_specs = [
        pl.BlockSpec((None, None, tq, LANES), lambda bi, p, i: (bi, p, i, 0)),
        pl.BlockSpec((None, nq, LANES, tq), lambda bi, p, i: (bi, 0, p, 0)),
        pl.BlockSpec((None, None, t, LANES), lambda bi, p, i: (bi, p, 0, 0)),
    ]
    out_spec = pl.BlockSpec((None, tq, LANES), lambda bi, p, i: (bi, i, p))
    if kind == 1:
        kern = functools.partial(_attn_b_kernel, tq=tq)
        consts = [_strict_lower(tq)]
        const_specs = [pl.BlockSpec((tq, tq), lambda bi, p, i: (0, 0))]
    else:
        lam_p, norm_g, lam_init = extra
        kern = functools.partial(_attn_c_kernel, tq=tq, ct=ct, lam_init=lam_init)
        consts = [lam_p, norm_g.reshape(1, LANES)]
        const_specs = [pl.BlockSpec(lam_p.shape, lambda bi, p, i: (0, 0)),
                       pl.BlockSpec((1, LANES), lambda bi, p, i: (0, 0))]
    return pl.pallas_call(
        kern,
        grid=(b, N_PAIRS, nq),
        in_specs=const_specs + qkv_specs,
        out_specs=out_spec,
        out_shape=out_shape,
        compiler_params=_cparams(("parallel", "parallel", "parallel"), 32),
        name=name,
    )(*consts, q, kt, v)


def _dec_idx_kernel(pt_ref, iq_ref, w_ref, ikn_ref, l_ref, *rest, pages, groups, n_sel):
    page_refs = rest[:pages]
    bias_ref = rest[pages]
    key_scr = rest[pages + 1]
    g = pl.program_id(1)
    ts = bias_ref.shape[1]
    width = pages * PAGE_SIZE
    iq = iq_ref[...]
    w = w_ref[...]

    def to_keys(logits):
        lg = jnp.maximum(logits, 0.0) * w
        s = lg[0:ts]
        for h in range(1, IDX_HEADS):
            s = s + lg[h * ts:(h + 1) * ts]
        return _sort_key(s)

    @pl.when(g == 0)
    def _new_keys():
        ikn = jnp.concatenate([ikn_ref[...], jnp.zeros((PAGE_SIZE - ts, IDX_DIM), F32)], axis=0)
        key = to_keys(_dot_nt(iq, ikn.astype(BF16)))
        t_idx = lax.broadcasted_iota(I32, (ts, PAGE_SIZE), 0)
        j_idx = lax.broadcasted_iota(I32, (ts, PAGE_SIZE), 1)
        key = jnp.where(j_idx <= t_idx, key, INT_MIN)
        key_scr[0] = jnp.concatenate(
            [key, jnp.full((ts, width - PAGE_SIZE), INT_MIN, I32)], axis=1)

    key_scr[g + 1] = jnp.concatenate(
        [to_keys(_dot(iq, r[...].astype(BF16))) for r in page_refs], axis=1)

    @pl.when(g == groups - 1)
    def _select():
        def count(pred):
            acc = jnp.zeros((ts, width), F32)
            for sl in range(groups + 1):
                acc = acc + jnp.where(pred(key_scr[sl]), 1.0, 0.0)
            return jnp.sum(acc, axis=-1, keepdims=True)

        def search(it, cur):
            cand = cur + jnp.left_shift(jnp.int32(1), 31 - it)
            return jnp.where(count(lambda key: key >= cand) >= n_sel, cand, cur)

        cur = lax.fori_loop(0, 32, search, jnp.full((ts, 1), INT_MIN, I32))
        thr = jnp.maximum(cur, INT_MIN + 1)
        ties = jnp.max(count(lambda key: key >= thr)) > n_sel

        @pl.when(jnp.logical_not(ties))
        def _by_threshold():
            for sl in range(groups + 1):
                bias_ref[sl] = jnp.where(key_scr[sl] >= thr, 0.0, NEG)

        @pl.when(ties)
        def _by_rank():
            need = n_sel - count(lambda key: key > thr)
            lmat = l_ref[...]
            seen = jnp.zeros((ts, 1), F32)
            for sl in list(range(1, groups + 1)) + [0]:
                key = key_scr[sl]
                eq = key == thr
                eqf = jnp.where(eq, 1.0, 0.0)
                eq16 = jnp.concatenate([eqf, jnp.zeros_like(eqf)], axis=0).astype(BF16)
                rank = _dot(eq16, lmat)[0:ts] + seen
                sel = (key > thr) | (eq & (rank < need))
                bias_ref[sl] = jnp.where(sel, 0.0, NEG)
                seen = seen + jnp.sum(eqf, axis=-1, keepdims=True)


def _dec_attn_kernel(pt_ref, *refs, mode, pages, groups, ts, lam_init):
    it = iter(refs)
    qbd_ref, kn_ref, vn_ref = next(it), next(it), next(it)
    bias_ref = next(it) if mode == "a" else None
    u_ref = next(it) if mode == "b" else None
    lam_ref, gn_ref = (next(it), next(it)) if mode == "c" else (None, None)
    kt_refs = [next(it) for _ in range(pages)]
    vt_refs = [next(it) for _ in range(pages)]
    o_ref = next(it)
    m_scr, l_scr, acc_scr = next(it), next(it), next(it)
    g = pl.program_id(1)
    rows = 2 * N_PAIRS * ts
    qbd = qbd_ref[...]

    def rows_of(x):
        return jnp.concatenate([x] * (2 * N_PAIRS), axis=0)

    def step(s, pvs, valid, u):
        if mode == "b":
            sp = _softplus(s)
            spm = sp if valid is None else jnp.where(valid, sp, 0.0)
            cs = _dot(spm.astype(BF16), u)
            w = jnp.exp(s - sp - (cs + m_scr[...]))
            if valid is not None:
                w = jnp.where(valid, w, 0.0)
            m_scr[...] += cs[:, 0:1] + spm[:, 0:1]
            p = w.astype(BF16)
            acc = acc_scr[...]
        else:
            if valid is not None:
                s = jnp.where(valid, s, NEG)
            m_old = m_scr[...]
            m_new = jnp.maximum(m_old, jnp.max(s, axis=-1, keepdims=True))
            a = jnp.exp(m_old - m_new)
            pf = jnp.exp(s - m_new)
            l_scr[...] = a * l_scr[...] + jnp.sum(pf, axis=-1, keepdims=True)
            m_scr[...] = m_new
            p = pf.astype(BF16)
            acc = a * acc_scr[...]
        for j, pv in enumerate(pvs):
            acc = acc + pv(p[:, j * PAGE_SIZE:(j + 1) * PAGE_SIZE])
        acc_scr[...] = acc

    @pl.when(g == 0)
    def _new_keys():
        m_scr[...] = jnp.full((rows, 1), 0.0 if mode == "b" else NEG, F32)
        l_scr[...] = jnp.zeros((rows, 1), F32)
        acc_scr[...] = jnp.zeros((rows, TOK_WIDTH), F32)
        pad = jnp.zeros((PAGE_SIZE - ts, TOK_WIDTH), F32)
        kn = jnp.concatenate([kn_ref[...], pad], axis=0).astype(BF16)
        vn = jnp.concatenate([vn_ref[...], pad], axis=0).astype(BF16)
        s = _dot_nt(qbd, kn)
        t_idx = rows_of(lax.broadcasted_iota(I32, (ts, PAGE_SIZE), 0))
        j_idx = lax.broadcasted_iota(I32, (rows, PAGE_SIZE), 1)
        valid = (j_idx < t_idx) if mode == "b" else (j_idx <= t_idx)
        if mode == "a":
            s = s + rows_of(bias_ref[:, 0:PAGE_SIZE])
        step(s, [lambda p: _dot(p, vn)], valid,
             u_ref[0:PAGE_SIZE, 0:PAGE_SIZE] if mode == "b" else None)

    @pl.when(g > 0)
    def _pages():
        s = jnp.concatenate([_dot(qbd, r[...].astype(BF16)) for r in kt_refs], axis=1)
        if mode == "a":
            s = s + rows_of(bias_ref[...])
        pvs = [functools.partial(lambda p, r: _dot_nt(p, r[...].astype(BF16)), r=r) for r in vt_refs]
        step(s, pvs, None, u_ref[...] if mode == "b" else None)

    @pl.when(g == groups)
    def _finish():
        acc = acc_scr[...]
        if mode != "b":
            acc = acc / l_scr[...]
        lo_half = _lane_iota() < HEAD_DIM
        pieces = []
        for p in range(N_PAIRS):
            blk = acc[2 * p * ts:(2 * p + 2) * ts, p * LANES:(p + 1) * LANES]
            first, second = blk[0:ts], blk[ts:2 * ts]
            if mode == "c":
                o = first - _diff_lambda(lam_ref[...], lam_init) * second
                o = o * lax.rsqrt(jnp.mean(o * o, axis=-1, keepdims=True) + RMS_EPS)
                pieces.append(o * gn_ref[...] * (1.0 - lam_init))
            else:
                pieces.append(jnp.where(lo_half, first, second))
        o_ref[...] = jnp.concatenate(pieces, axis=1)


def _pages_per_step(n_pages):
    p = 8
    while n_pages % p:
        p //= 2
    return p


def _dec_indexer(page_table, cache_ikt, slot, iq_rows, w_rows, ik_new, n_sel, name):
    bs, n_pages = page_table.shape
    ts = ik_new.shape[1]
    pages = _pages_per_step(n_pages)
    groups = n_pages // pages
    width = pages * PAGE_SIZE
    lmat = _strict_lower(width).T

    def page_spec(p):
        return pl.BlockSpec((None, None, IDX_DIM, PAGE_SIZE),
                            lambda b, g, pt: (slot, pt[b, g * pages + p], 0, 0))

    grid_spec = pltpu.PrefetchScalarGridSpec(
        num_scalar_prefetch=1,
        grid=(bs, groups),
        in_specs=[
            pl.BlockSpec((None, IDX_HEADS * ts, IDX_DIM), lambda b, g, pt: (b, 0, 0)),
            pl.BlockSpec((None, IDX_HEADS * ts, 1), lambda b, g, pt: (b, 0, 0)),
            pl.BlockSpec((None, ts, IDX_DIM), lambda b, g, pt: (b, 0, 0)),
            pl.BlockSpec((width, width), lambda b, g, pt: (0, 0)),
        ] + [page_spec(p) for p in range(pages)],
        out_specs=pl.BlockSpec((None, groups + 1, ts, width), lambda b, g, pt: (b, 0, 0, 0)),
        scratch_shapes=[pltpu.VMEM((groups + 1, ts, width), I32)],
    )
    kern = functools.partial(_dec_idx_kernel, pages=pages, groups=groups, n_sel=float(n_sel))
    return pl.pallas_call(
        kern,
        grid_spec=grid_spec,
        out_shape=jax.ShapeDtypeStruct((bs, groups + 1, ts, width), F32),
        compiler_params=_cparams(("parallel", "arbitrary"), 32),
        name=name,
    )(page_table, iq_rows, w_rows, ik_new, lmat, *([cache_ikt] * pages))


def _dec_mixer(mode, layer, page_table, cache_kt, cache_vt, qbd, k_new, v_new, *, extra, name):
    bs, n_pages = page_table.shape
    ts = k_new.shape[1]
    pages = _pages_per_step(n_pages)
    groups = n_pages // pages
    width = pages * PAGE_SIZE
    rows = 2 * N_PAIRS * ts
    descending = mode == "b"

    def page_spec(p):
        def imap(b, g, pt):
            grp = jnp.maximum(g - 1, 0)
            if descending:
                grp = groups - 1 - grp
            return (layer, pt[b, grp * pages + p], 0, 0)
        return pl.BlockSpec((None, None, TOK_WIDTH, PAGE_SIZE), imap)

    in_specs = [
        pl.BlockSpec((None, rows, TOK_WIDTH), lambda b, g, pt: (b, 0, 0)),
        pl.BlockSpec((None, ts, TOK_WIDTH), lambda b, g, pt: (b, 0, 0)),
        pl.BlockSpec((None, ts, TOK_WIDTH), lambda b, g, pt: (b, 0, 0)),
    ]
    args = [qbd, k_new, v_new]
    lam_init = 0.0
    if mode == "a":
        in_specs.append(pl.BlockSpec((None, None, ts, width), lambda b, g, pt: (b, g, 0, 0)))
        args.append(extra)
    elif mode == "b":
        in_specs.append(pl.BlockSpec((width, width), lambda b, g, pt: (0, 0)))
        args.append(_strict_lower(width))
    else:
        lam_p, norm_g, lam_init = extra
        in_specs += [pl.BlockSpec(lam_p.shape, lambda b, g, pt: (0, 0)),
                     pl.BlockSpec((1, LANES), lambda b, g, pt: (0, 0))]
        args += [lam_p, norm_g.reshape(1, LANES)]
    in_specs += [page_spec(p) for p in range(pages)] * 2
    args += [cache_kt] * pages + [cache_vt] * pages
    grid_spec = pltpu.PrefetchScalarGridSpec(
        num_scalar_prefetch=1,
        grid=(bs, groups + 1),
        in_specs=in_specs,
        out_specs=pl.BlockSpec((None, ts, TOK_WIDTH), lambda b, g, pt: (b, 0, 0)),
        scratch_shapes=[pltpu.VMEM((rows, 1), F32), pltpu.VMEM((rows, 1), F32),
                        pltpu.VMEM((rows, TOK_WIDTH), F32)],
    )
    kern = functools.partial(_dec_attn_kernel, mode=mode, pages=pages, groups=groups, ts=ts,
                             lam_init=lam_init)
    return pl.pallas_call(
        kern,
        grid_spec=grid_spec,
        out_shape=jax.ShapeDtypeStruct((bs, ts, TOK_WIDTH), F32),
        compiler_params=_cparams(("parallel", "arbitrary"), 48),
        name=name,
    )(page_table, *args)


def _oproj_ln_kernel(tok_ref, mem_ref, x_ref, wt_ref, wm_ref, g_ref, b_ref, o_ref, obf_ref, *, alpha):
    mix = (_dot(tok_ref[...].astype(BF16), wt_ref[...])
           + _dot(mem_ref[...].astype(BF16), wm_ref[...]))
    out = _layer_norm(alpha * x_ref[...] + mix, g_ref[...], b_ref[...])
    o_ref[...] = out
    obf_ref[...] = out.astype(BF16)


def _oproj_ln(tok, mem, x, w_tok, w_mem, g, b, alpha, name):
    m_rows, d = x.shape
    tm = _tiles(m_rows)["proj"]
    row = lambda n: pl.BlockSpec((tm, n), lambda i: (i, 0))
    full = lambda a: pl.BlockSpec(a.shape, lambda i: (0, 0))
    return pl.pallas_call(
        functools.partial(_oproj_ln_kernel, alpha=alpha),
        grid=(m_rows // tm,),
        in_specs=[row(TOK_WIDTH), row(MEM_WIDTH), row(d), full(w_tok), full(w_mem), full(g), full(b)],
        out_specs=[row(d), row(d)],
        out_shape=[jax.ShapeDtypeStruct((m_rows, d), F32), jax.ShapeDtypeStruct((m_rows, d), BF16)],
        compiler_params=_cparams(("parallel",), 40),
        name=name,
    )(tok, mem, x, w_tok, w_mem, g, b)


def _ffn_ln_kernel(xbf_ref, x_ref, wg_ref, wu_ref, wd_ref, g_ref, b_ref, o_ref, obf_ref, acc_ref,
                   *, alpha, n_f):
    f = pl.program_id(1)
    xb = xbf_ref[...]
    hg = _dot(xb, wg_ref[...])
    hu = _dot(xb, wu_ref[...])
    act = hg * (1.0 / (1.0 + jnp.exp(-hg))) * hu
    part = _dot(act.astype(BF16), wd_ref[...])

    @pl.when(f == 0)
    def _first():
        acc_ref[...] = part

    @pl.when(f > 0)
    def _rest():
        acc_ref[...] += part

    @pl.when(f == n_f - 1)
    def _finish():
        out = _layer_norm(alpha * x_ref[...] + acc_ref[...], g_ref[...], b_ref[...])
        o_ref[...] = out
        obf_ref[...] = out.astype(BF16)


def _ffn_ln(xbf, x, w_gu, w_d, g, b, alpha, name):
    m_rows, d = x.shape
    d_ff = w_d.shape[0]
    tm = _tiles(m_rows)["ffn"]
    n_f = 2
    tf = d_ff // n_f
    row = lambda: pl.BlockSpec((tm, d), lambda i, f: (i, 0))
    vec = lambda a: pl.BlockSpec(a.shape, lambda i, f: (0, 0))
    return pl.pallas_call(
        functools.partial(_ffn_ln_kernel, alpha=alpha, n_f=n_f),
        grid=(m_rows // tm, n_f),
        in_specs=[row(), row(),
                  pl.BlockSpec((d, tf), lambda i, f: (0, f)),
                  pl.BlockSpec((d, tf), lambda i, f: (0, f + n_f)),
                  pl.BlockSpec((tf, d), lambda i, f: (f, 0)),
                  vec(g), vec(b)],
        out_specs=[row(), row()],
        out_shape=[jax.ShapeDtypeStruct((m_rows, d), F32), jax.ShapeDtypeStruct((m_rows, d), BF16)],
        scratch_shapes=[pltpu.VMEM((tm, d), F32)],
        compiler_params=_cparams(("parallel", "arbitrary"), 56),
        name=name,
    )(xbf, x, w_gu, w_gu, w_d, g, b)


def _rope_tables(pos):
    half = HEAD_DIM // 2
    inv = ROPE_THETA ** (-jnp.arange(half, dtype=F32) / half)
    ang = pos.astype(F32)[:, None] * inv[None, :]
    cos = jnp.cos(ang)
    sin = jnp.sin(ang)
    return jnp.tile(cos, (1, 4)), jnp.tile(jnp.concatenate([-sin, sin], axis=1), (1, 2))


def _heads_last(xt, n_heads):
    l, b, _, p = xt.shape
    return xt.reshape(l, b, n_heads, HEAD_DIM, p).transpose(0, 1, 4, 2, 3)


def kernel(x_prompt, x_sample, mem_prompt, cache_k, cache_v, cache_idx_k, cache_mem_k, cache_mem_v, page_table, w_in, w_idx, w_mem_kv, w_o, ln_mix_g, ln_mix_b, w_gate_up, w_down, ln_ffn_g, ln_ffn_b, diff_lambda, diff_norm_g):
    bp, tp, d_model = x_prompt.shape
    bs, ts, _ = x_sample.shape
    n_pages = page_table.shape[1]
    past = n_pages * PAGE_SIZE
    depth = w_in.shape[0]
    n_mem = mem_prompt.shape[1]
    n_pool = cache_k.shape[1]
    n_heads = 2 * N_PAIRS
    mem_heads = MEM_WIDTH // HEAD_DIM
    alpha = (2 * depth) ** 0.25
    nsel_p = min(TOPK_MAX, tp // 4)
    nsel_s = min(TOPK_MAX, (past + ts) // 4)
    mp, ms = bp * tp, bs * ts

    w_in_b = w_in.astype(BF16)
    w_mkv_b = w_mem_kv.astype(BF16)
    w_o_b = w_o.astype(BF16)
    w_gu_b = w_gate_up.astype(BF16)
    w_d_b = w_down.astype(BF16)
    iq_w, ik_w, ih_w = (w_idx[..., :IDX_Q_WIDTH], w_idx[..., IDX_Q_WIDTH:IDX_Q_WIDTH + IDX_DIM],
                        w_idx[..., IDX_Q_WIDTH + IDX_DIM:])
    w_idx_b = jnp.concatenate(
        [iq_w, ik_w, ik_w, ih_w, jnp.zeros(ih_w.shape[:2] + (LANES - IDX_HEADS,), F32)],
        axis=-1).astype(BF16)
    idx_cols = IDX_Q_WIDTH + 2 * LANES

    rope_p = _rope_tables(jnp.arange(tp))
    rope_s = _rope_tables(jnp.tile(past + jnp.arange(ts), bs))
    cache_kt = cache_k.transpose(0, 1, 3, 4, 2).reshape(depth, n_pool, TOK_WIDTH, PAGE_SIZE)
    cache_vt = cache_v.transpose(0, 1, 3, 4, 2).reshape(depth, n_pool, TOK_WIDTH, PAGE_SIZE)
    cache_ikt = cache_idx_k.transpose(0, 1, 3, 2)
    cache_mkt = cache_mem_k.transpose(0, 1, 3, 4, 2).reshape(depth, bs, MEM_WIDTH, n_mem)
    cache_mvt = cache_mem_v.transpose(0, 1, 3, 4, 2).reshape(depth, bs, MEM_WIDTH, n_mem)
    mem2d = mem_prompt.reshape(bp * n_mem, d_model)
    head_of_col = jnp.arange(TOK_WIDTH) // HEAD_DIM
    bd_mask = (jnp.arange(n_heads)[:, None, None] == head_of_col[None, None, :])

    xp = x_prompt.reshape(mp, d_model)
    xs = x_sample.reshape(ms, d_model)
    xp_in, xs_in = xp, xs
    nidx_p, nmk_p, nmv_p, nk_s, nv_s, nidx_s = [], [], [], [], [], []
    kv_stack = None

    for l in range(depth):
        kind = l % N_MIXERS
        slot = l // N_MIXERS
        n_rope = n_heads if kind != 1 else 0
        tag = f"l{l}"
        c_q, c_k, c_v, c_m = 0, TOK_WIDTH, 2 * TOK_WIDTH, 3 * TOK_WIDTH
        in_w = c_m + MEM_WIDTH

        q_p, kt_f, kt_p, vt_f, v_p, mq_p = _proj(
            xp_in, w_in_b[l],
            (("pair", c_q, c_k, QK_SCALE, BF16), ("Tstack", c_k, c_v, 1.0, F32),
             ("Ttile", c_k, c_v, 1.0, BF16), ("Tstack", c_v, c_m, 1.0, F32),
             ("pair", c_v, c_m, 1.0, BF16), ("flat", c_m, in_w, QK_SCALE, BF16)),
            batch=bp, rope=rope_p, n_rope_chunks=n_rope, stack=(depth, l, kv_stack),
            name=f"proj_p_{tag}")
        kv_stack = (kt_f, vt_f)
        mkt_f, mvt_f = _proj(
            mem2d, w_mkv_b[l],
            (("T", 0, MEM_WIDTH, 1.0, F32), ("T", MEM_WIDTH, 2 * MEM_WIDTH, 1.0, F32)),
            batch=bp, name=f"proj_mem_{tag}")
        nmk_p.append(mkt_f)
        nmv_p.append(mvt_f)
        mem_p = _mem_attn(mq_p.reshape(bp, tp, MEM_WIDTH), mkt_f, mvt_f, None, BF16, f"mem_p_{tag}")

        q_s, k_s, v_s, mq_s = _proj(
            xs_in, w_in_b[l],
            (("flat", c_q, c_k, QK_SCALE, F32), ("flat", c_k, c_v, 1.0, F32),
             ("flat", c_v, c_m, 1.0, F32), ("flat", c_m, in_w, QK_SCALE, F32)),
            batch=bs, rope=rope_s, n_rope_chunks=n_rope, name=f"proj_s_{tag}")
        k_s3 = k_s.reshape(bs, ts, TOK_WIDTH)
        v_s3 = v_s.reshape(bs, ts, TOK_WIDTH)
        nk_s.append(k_s3.reshape(bs, ts, n_heads, HEAD_DIM))
        nv_s.append(v_s3.reshape(bs, ts, n_heads, HEAD_DIM))
        qbd = jnp.where(bd_mask[None], q_s.reshape(bs, 1, ts, TOK_WIDTH), 0.0)
        qbd = qbd.reshape(bs, n_heads * ts, TOK_WIDTH).astype(BF16)
        mem_s = _mem_attn(mq_s.reshape(bs, ts, MEM_WIDTH), cache_mkt, cache_mvt, l, F32,
                          f"mem_s_{tag}")

        if kind == 0:
            iq_p, ikt_p, ikt_f, iw_p = _proj(
                xp_in, w_idx_b[slot],
                (("flat", 0, IDX_Q_WIDTH, 1.0, BF16),
                 ("Ttile", IDX_Q_WIDTH, IDX_Q_WIDTH + LANES, 1.0, BF16),
                 ("T", IDX_Q_WIDTH, IDX_Q_WIDTH + IDX_DIM, 1.0, F32),
                 ("flat", IDX_Q_WIDTH + LANES, idx_cols, IDX_W_SCALE, F32)),
                batch=bp, rope=rope_p, n_rope_chunks=IDX_Q_WIDTH // LANES + 1, name=f"proj_ip_{tag}")
            nidx_p.append(ikt_f)
            tok_p = _prompt_mixer(
                0, q_p, kt_p, v_p,
                extra=(iq_p.reshape(bp, tp, IDX_Q_WIDTH), ikt_p, iw_p.reshape(bp, tp, LANES), nsel_p),
                name=f"mix_a_{tag}")

            iq_s, ik_s, iw_s = _proj(
                xs_in, w_idx_b[slot],
                (("flat", 0, IDX_Q_WIDTH, 1.0, F32),
                 ("flat", IDX_Q_WIDTH, IDX_Q_WIDTH + IDX_DIM, 1.0, F32),
                 ("flat", IDX_Q_WIDTH + LANES, idx_cols, IDX_W_SCALE, F32)),
                batch=bs, rope=rope_s, n_rope_chunks=IDX_Q_WIDTH // LANES + 1, name=f"proj_is_{tag}")
            ik_s3 = ik_s.reshape(bs, ts, IDX_DIM)
            nidx_s.append(ik_s3)
            iq_rows = iq_s.reshape(bs, ts, IDX_HEADS, IDX_DIM).swapaxes(1, 2)
            iq_rows = iq_rows.reshape(bs, IDX_HEADS * ts, IDX_DIM).astype(BF16)
            w_rows = iw_s.reshape(bs, ts, LANES)[:, :, :IDX_HEADS].swapaxes(1, 2)
            w_rows = w_rows.reshape(bs, IDX_HEADS * ts, 1)
            bias = _dec_indexer(page_table, cache_ikt, slot, iq_rows, w_rows, ik_s3, nsel_s,
                                f"idx_s_{tag}")
            tok_s = _dec_mixer("a", l, page_table, cache_kt, cache_vt, qbd, k_s3, v_s3,
                               extra=bias, name=f"mix_sa_{tag}")
        elif kind == 1:
            tok_p = _prompt_mixer(1, q_p, kt_p, v_p, extra=None, name=f"mix_b_{tag}")
            tok_s = _dec_mixer("b", l, page_table, cache_kt, cache_vt, qbd, k_s3, v_s3,
                               extra=None, name=f"mix_sb_{tag}")
        else:
            lam_init = 0.8 - 0.6 * math.exp(-0.3 * l)
            extra = (diff_lambda[slot], diff_norm_g[slot], lam_init)
            tok_p = _prompt_mixer(2, q_p, kt_p, v_p, extra=extra, name=f"mix_c_{tag}")
            tok_s = _dec_mixer("c", l, page_table, cache_kt, cache_vt, qbd, k_s3, v_s3,
                               extra=extra, name=f"mix_sc_{tag}")

        w_tok, w_mem = w_o_b[l, :TOK_WIDTH], w_o_b[l, TOK_WIDTH:]
        g1, b1 = ln_mix_g[l].reshape(1, d_model), ln_mix_b[l].reshape(1, d_model)
        g2, b2 = ln_ffn_g[l].reshape(1, d_model), ln_ffn_b[l].reshape(1, d_model)
        x1, x1b = _oproj_ln(tok_p.reshape(mp, TOK_WIDTH), mem_p.reshape(mp, MEM_WIDTH), xp,
                            w_tok, w_mem, g1, b1, alpha, f"oproj_p_{tag}")
        xp, xp_in = _ffn_ln(x1b, x1, w_gu_b[l], w_d_b[l], g2, b2, alpha, f"ffn_p_{tag}")
        y1, y1b = _oproj_ln(tok_s.reshape(ms, TOK_WIDTH), mem_s.reshape(ms, MEM_WIDTH), xs,
                            w_tok, w_mem, g1, b1, alpha, f"oproj_s_{tag}")
        xs, xs_in = _ffn_ln(y1b, y1, w_gu_b[l], w_d_b[l], g2, b2, alpha, f"ffn_s_{tag}")

    return (xp.reshape(bp, tp, d_model), xs.reshape(bs, ts, d_model),
            _heads_last(kv_stack[0], n_heads), _heads_last(kv_stack[1], n_heads),
            jnp.stack(nidx_p).transpose(0, 1, 3, 2),
            _heads_last(jnp.stack(nmk_p), mem_heads), _heads_last(jnp.stack(nmv_p), mem_heads),
            jnp.stack(nk_s), jnp.stack(nv_s), jnp.stack(nidx_s))
```

```python
import functools
import math

import jax
import jax.numpy as jnp
from jax import lax
from jax.experimental import pallas as pl
from jax.experimental.pallas import tpu as pltpu

F32 = jnp.float32
BF16 = jnp.bfloat16
I32 = jnp.int32

HEAD_DIM = 64
TOK_WIDTH = 768
N_PAIRS = 6
MEM_WIDTH = 256
IDX_HEADS = 8
IDX_DIM = 64
IDX_Q_WIDTH = IDX_HEADS * IDX_DIM
IDX_W_SCALE = IDX_HEADS ** -0.5 * IDX_DIM ** -0.5
PAGE_SIZE = 128
TOPK_MAX = 256
N_MIXERS = 3
ROPE_THETA = 10000.0
LN_EPS = 1e-5
RMS_EPS = 1e-5
QK_SCALE = HEAD_DIM ** -0.5

LANES = 128
NEG = -1e30
STICK_EXIT = 110.0
INT_MIN = -2 ** 31
HALF16 = 2 ** 15
I16 = jnp.int16
MIB = 1024 * 1024


def _tiles(rows):
    attn = min(256, rows)
    return dict(proj=min(512, rows), ffn=min(512, rows), attn=attn, chunk=min(4, rows // attn))


def _cparams(sem, vmem_mib):
    return pltpu.CompilerParams(dimension_semantics=sem, vmem_limit_bytes=vmem_mib * MIB)


def _dot(a, b):
    return jnp.dot(a, b, preferred_element_type=F32)


def _dot_nt(a, b):
    return lax.dot_general(a, b, (((1,), (1,)), ((), ())), preferred_element_type=F32)


def _lane_iota():
    return lax.broadcasted_iota(I32, (1, LANES), 1)


def _layer_norm(y, g, b):
    mu = jnp.mean(y, axis=-1, keepdims=True)
    d = y - mu
    var = jnp.mean(d * d, axis=-1, keepdims=True)
    return d * lax.rsqrt(var + LN_EPS) * g + b


def _softmax_update(s, m, l, acc, pv):
    m_new = jnp.maximum(m, jnp.max(s, axis=-1, keepdims=True))
    a = jnp.exp(m - m_new)
    p = jnp.exp(s - m_new)
    l = a * l + jnp.sum(p, axis=-1, keepdims=True)
    acc = a * acc + pv(p.astype(BF16))
    return m_new, l, acc


def _sort_key(x):
    bits = lax.bitcast_convert_type(x, I32)
    return bits ^ ((bits >> 31) & 0x7FFFFFFF)


def _softplus(z):
    return jnp.maximum(z, 0.0) + jnp.log(1.0 + jnp.exp(-jnp.abs(z)))


def _diff_lambda(lp, lam_init):
    return (jnp.exp(jnp.sum(lp[0:1] * lp[1:2], axis=-1, keepdims=True))
            - jnp.exp(jnp.sum(lp[2:3] * lp[3:4], axis=-1, keepdims=True)) + lam_init)


def _head_split(q):
    lo_half = _lane_iota() < HEAD_DIM
    zero = jnp.zeros_like(q)
    return jnp.where(lo_half, q, zero), jnp.where(lo_half, zero, q)


def _proj_kernel(*refs, n_rope_chunks, outs, tk):
    x_ref, w_ref = refs[:2]
    if n_rope_chunks:
        cos_ref, sin_ref = refs[2:4]
    o_refs = refs[len(refs) - len(outs):]
    y = _dot(x_ref[...].astype(BF16), w_ref[...])
    n_chunks = y.shape[1] // LANES
    chunks = [y[:, c * LANES:(c + 1) * LANES] for c in range(n_chunks)]
    if n_rope_chunks:
        cos = cos_ref[...]
        sin = sin_ref[...]
        first = (_lane_iota() & (HEAD_DIM - 1)) < HEAD_DIM // 2
        for c in range(n_rope_chunks):
            yc = chunks[c]
            partner = jnp.where(first, pltpu.roll(yc, LANES - HEAD_DIM // 2, 1),
                                pltpu.roll(yc, HEAD_DIM // 2, 1))
            chunks[c] = yc * cos + partner * sin
    for o_ref, (kind, lo, hi, scale) in zip(o_refs, outs):
        c0, c1 = lo // LANES, -(-hi // LANES)
        if kind == "pair":
            for p in range(c1 - c0):
                v = chunks[c0 + p]
                o_ref[p] = (v * scale if scale != 1.0 else v).astype(o_ref.dtype)
            continue
        v = chunks[c0] if c1 - c0 == 1 else jnp.concatenate(chunks[c0:c1], axis=1)
        if scale != 1.0:
            v = v * scale
        if kind == "flat":
            o_ref[...] = v[:, lo - c0 * LANES:hi - c0 * LANES].astype(o_ref.dtype)
        elif kind == "T":
            o_ref[...] = v.T[0:hi - lo].astype(o_ref.dtype)
        else:
            vt = v.T.astype(o_ref.dtype)
            for j in range(vt.shape[1] // tk):
                o_ref[j] = vt[:, j * tk:(j + 1) * tk]


def _proj(x, w, outs, *, batch, rope=None, n_rope_chunks=0, name):
    m_rows, k_dim = x.shape
    n_cols = w.shape[1]
    t_len = m_rows // batch
    structured = any(o[0] != "flat" for o in outs)
    tm = _tiles(t_len if structured else m_rows)["proj"]
    tk = _tiles(t_len)["attn"]
    tiles_per_seq = max(t_len // tm, 1)
    in_specs = [pl.BlockSpec((tm, k_dim), lambda i: (i, 0)),
                pl.BlockSpec((k_dim, n_cols), lambda i: (0, 0))]
    args = [x, w]
    if n_rope_chunks:
        cos, sin = rope
        n_tab = cos.shape[0] // tm
        in_specs += [pl.BlockSpec((tm, LANES), lambda i: (i % n_tab, 0))] * 2
        args += [cos, sin]
    seq_of = lambda i: i // tiles_per_seq
    tile_of = lambda i: i % tiles_per_seq
    out_shapes, out_specs = [], []
    for kind, lo, hi, _, dtype in outs:
        if kind == "pair":
            groups = (hi - lo) // LANES
            shape, block = (batch, groups, t_len, LANES), (None, groups, tm, LANES)
            imap = lambda i: (seq_of(i), 0, tile_of(i), 0)
        elif kind == "T":
            shape, block = (batch, hi - lo, t_len), (None, hi - lo, tm)
            imap = lambda i: (seq_of(i), 0, tile_of(i))
        elif kind == "Ttile":
            shape, block = (batch, t_len // tk, hi - lo, tk), (None, tm // tk, hi - lo, tk)
            imap = lambda i: (seq_of(i), tile_of(i), 0, 0)
        else:
            shape, block = (m_rows, hi - lo), (tm, hi - lo)
            imap = lambda i: (i, 0)
        out_shapes.append(jax.ShapeDtypeStruct(shape, dtype))
        out_specs.append(pl.BlockSpec(block, imap))
    kern = functools.partial(_proj_kernel, n_rope_chunks=n_rope_chunks,
                             outs=tuple(o[:4] for o in outs), tk=tk)
    return pl.pallas_call(
        kern,
        grid=(m_rows // tm,),
        in_specs=in_specs,
        out_specs=out_specs,
        out_shape=out_shapes,
        compiler_params=_cparams(("parallel",), 56),
        name=name,
    )(*args)


def _mem_attn_kernel(q_ref, kt_ref, vt_ref, o_ref):
    lo_half = _lane_iota() < HEAD_DIM
    pieces = []
    for pr in range(MEM_WIDTH // LANES):
        sl = slice(pr * LANES, (pr + 1) * LANES)
        kt = kt_ref[sl, :].astype(BF16)
        vt = vt_ref[sl, :].astype(BF16)
        res = []
        for qc in _head_split(q_ref[:, sl].astype(BF16)):
            s = _dot(qc, kt)
            m = jnp.max(s, axis=-1, keepdims=True)
            p = jnp.exp(s - m)
            l = jnp.sum(p, axis=-1, keepdims=True)
            res.append(_dot_nt(p.astype(BF16), vt) / l)
        pieces.append(jnp.where(lo_half, res[0], res[1]))
    o_ref[...] = jnp.concatenate(pieces, axis=1).astype(o_ref.dtype)


def _mem_attn(q, kt, vt, layer, out_dtype, name):
    b, t, _ = q.shape
    tq = _tiles(t)["attn"]
    n_mem = kt.shape[-1]
    if layer is None:
        kv_spec = pl.BlockSpec((None, MEM_WIDTH, n_mem), lambda bi, i: (bi, 0, 0))
    else:
        kv_spec = pl.BlockSpec((None, None, MEM_WIDTH, n_mem), lambda bi, i: (layer, bi, 0, 0))
    return pl.pallas_call(
        _mem_attn_kernel,
        grid=(b, t // tq),
        in_specs=[pl.BlockSpec((None, tq, MEM_WIDTH), lambda bi, i: (bi, i, 0)), kv_spec, kv_spec],
        out_specs=pl.BlockSpec((None, tq, MEM_WIDTH), lambda bi, i: (bi, i, 0)),
        out_shape=jax.ShapeDtypeStruct((b, t, MEM_WIDTH), out_dtype),
        compiler_params=_cparams(("parallel", "parallel"), 32),
        name=name,
    )(q, kt, vt)


def _tile_iotas(t):
    return (lax.broadcasted_iota(I32, (t, t), 0), lax.broadcasted_iota(I32, (t, t), 1))


def _attn_c_kernel(lam_ref, g_ref, q_ref, kt_ref, v_ref, o_ref, *, tq, ct, lam_init):
    i = pl.program_id(2)
    diag_chunk = i // ct
    qs = _head_split(q_ref[...])

    row, col = _tile_iotas(tq)
    one_bf = jnp.ones((tq, LANES), BF16)

    def chunk(c, carry, diag):
        carry = list(carry)
        for j in range(ct):
            kt = c * ct + j
            k = kt_ref[kt]
            v = v_ref[pl.ds(pl.multiple_of(kt * tq, tq), tq), :]
            v1 = jnp.concatenate([v, one_bf], axis=1)
            for h in range(2):
                m, acc = carry[h]
                s = _dot(qs[h], k)
                if diag:
                    s = jnp.where(col + kt * tq <= row + i * tq, s, NEG)
                m_new = jnp.maximum(m, jnp.max(s, axis=-1, keepdims=True))
                p = jnp.exp(s - m_new).astype(BF16)
                carry[h] = (m_new, jnp.exp(m - m_new) * acc + _dot(p, v1))
        return tuple(carry)

    one = (jnp.full((tq, 1), NEG, F32), jnp.zeros((tq, 2 * LANES), F32))
    carry = lax.fori_loop(0, diag_chunk, lambda c, cr: chunk(c, cr, False), (one, one))
    (_, acc0), (_, acc1) = chunk(diag_chunk, carry, True)
    o = (acc0[:, :LANES] / acc0[:, LANES:]
         - _diff_lambda(lam_ref[...], lam_init) * (acc1[:, :LANES] / acc1[:, LANES:]))
    o = o * lax.rsqrt(jnp.mean(o * o, axis=-1, keepdims=True) + RMS_EPS)
    o_ref[...] = (o * g_ref[...] * (1.0 - lam_init)).astype(o_ref.dtype)


def _attn_b_kernel(u_ref, q_ref, kt_ref, v_ref, o_ref, *, tq):
    i = pl.program_id(2)
    qs = _head_split(q_ref[...])
    u = u_ref[...]
    row, col = _tile_iotas(tq)
    strict = col < row

    def tile(kt, carry, diag):
        k = kt_ref[kt]
        v = v_ref[pl.ds(pl.multiple_of(kt * tq, tq), tq), :]
        new = []
        for c in range(2):
            later, acc = carry[c]
            z = _dot(qs[c], k)
            sp = _softplus(z)
            spm = jnp.where(strict, sp, 0.0) if diag else sp
            cs = _dot(spm.astype(BF16), u)
            w = jnp.exp(z - sp - (cs + later))
            if diag:
                w = jnp.where(strict, w, 0.0)
            acc = acc + _dot(w.astype(BF16), v)
            later = later + cs[:, 0:1] + spm[:, 0:1]
            new.append((later, acc))
        return tuple(new)

    def live(carry):
        return (jnp.minimum(jnp.min(carry[0][0]), jnp.min(carry[1][0])) < STICK_EXIT).astype(I32)

    def cond(state):
        return (state[0] >= 0) & (state[1] > 0)

    def body(state):
        carry = tile(state[0], state[2], False)
        return state[0] - 1, live(carry), carry

    one = (jnp.zeros((tq, 1), F32), jnp.zeros((tq, LANES), F32))
    carry = tile(i, (one, one), True)
    carry = lax.while_loop(cond, body, (i - 1, live(carry), carry))[2]
    lo_half = _lane_iota() < HEAD_DIM
    o_ref[...] = jnp.where(lo_half, carry[0][1], carry[1][1]).astype(o_ref.dtype)


def _fold_lanes(c):
    out = c[:, 0:LANES]
    for j in range(1, c.shape[1] // LANES):
        out = out + c[:, j * LANES:(j + 1) * LANES]
    return out


def _attn_a_kernel(l_ref, iq_ref, ikt_ref, iw_ref, q_ref, kt_ref, v_ref, o_ref, key_scr, bias_scr,
                   hi_scr, lo_scr, *, tq, ct, n_sel):
    i = pl.program_id(1)
    pair = pl.program_id(2)
    n_chunks = i // ct + 1
    row, col = _tile_iotas(tq)
    causal = col <= row

    @pl.when(pair == 0)
    def _select():
        w_all = iw_ref[...]
        iq_heads = []
        for pr in range(IDX_Q_WIDTH // LANES):
            iq_heads += list(_head_split(iq_ref[:, pr * LANES:(pr + 1) * LANES]))

        def score_tile(kt, diag):
            ik = ikt_ref[kt]
            sc = jnp.zeros((tq, tq), F32)
            for h in range(IDX_HEADS):
                sc = sc + w_all[:, h:h + 1] * jnp.maximum(_dot(iq_heads[h], ik), 0.0)
            key = _sort_key(sc)
            if diag:
                key = jnp.where(causal, key, INT_MIN)
            key_scr[kt] = key
            hi_scr[kt] = (key >> 16).astype(I16)
            lo_scr[kt] = ((key & 0xFFFF) - HALF16).astype(I16)

        def _score_body(kt, c):
            score_tile(kt, False)
            return c

        lax.fori_loop(0, i, _score_body, 0)
        score_tile(i, True)

        def _pad_body(kt, c):
            bias_scr[kt] = jnp.full((tq, tq), NEG, F32)
            return c

        lax.fori_loop(i + 1, n_chunks * ct, _pad_body, 0)

        def count(pred):
            def body(kt, acc):
                return acc + _fold_lanes(jnp.where(pred(key_scr[kt]), 1.0, 0.0))
            acc = lax.fori_loop(0, i + 1, body, jnp.zeros((tq, LANES), F32))
            return jnp.sum(acc, axis=-1, keepdims=True)

        one16, zero16 = jnp.int16(1), jnp.int16(0)

        def count16(scr, bound, strict):
            b16 = jnp.broadcast_to(bound, (tq, tq)).astype(I16)

            def body(kt, acc):
                hit = (scr[kt] > b16) if strict else (scr[kt] >= b16)
                return acc + jnp.where(hit, one16, zero16)

            acc = lax.fori_loop(0, i + 1, body, jnp.zeros((tq, tq), I16))
            return jnp.sum(_fold_lanes(acc.astype(F32)), axis=-1, keepdims=True)

        def kth_largest16(scr, k):
            def step(it, cur):
                cand = cur + jnp.left_shift(jnp.int32(1), 15 - it)
                return jnp.where(count16(scr, cand, False) >= k, cand, cur)
            return lax.fori_loop(0, 16, step, jnp.full((tq, 1), -HALF16, I32))

        hi_thr = kth_largest16(hi_scr, n_sel)
        rank_lo = n_sel - count16(hi_scr, hi_thr, True)
        hi_thr16 = jnp.broadcast_to(hi_thr, (tq, tq)).astype(I16)

        def _keep_body(kt, c):
            lo_scr[kt] = jnp.where(hi_scr[kt] == hi_thr16, lo_scr[kt], jnp.int16(-HALF16))
            return c

        lax.fori_loop(0, i + 1, _keep_body, 0)
        lo_thr = kth_largest16(lo_scr, rank_lo)
        cur = jnp.left_shift(hi_thr, 16) + (lo_thr + HALF16)
        thr = jnp.maximum(cur, INT_MIN + 1)
        ties = jnp.max(count(lambda key: key >= thr)) > n_sel

        @pl.when(jnp.logical_not(ties))
        def _by_threshold():
            def body(kt, c):
                bias_scr[kt] = jnp.where(key_scr[kt] >= thr, 0.0, NEG)
                return c
            lax.fori_loop(0, i + 1, body, 0)

        @pl.when(ties)
        def _by_rank():
            need = n_sel - count(lambda key: key > thr)
            lmat = l_ref[...]

            def body(kt, seen):
                key = key_scr[kt]
                eq = key == thr
                eqf = jnp.where(eq, 1.0, 0.0)
                rank = _dot(eqf.astype(BF16), lmat) + seen
                sel = (key > thr) | (eq & (rank < need))
                bias_scr[kt] = jnp.where(sel, 0.0, NEG)
                return seen + jnp.sum(eqf, axis=-1, keepdims=True)

            lax.fori_loop(0, i + 1, body, jnp.zeros((tq, 1), F32))

    qs = _head_split(q_ref[...])
    lo_half = _lane_iota() < HEAD_DIM
    one_bf = jnp.ones((tq, LANES), BF16)

    def chunk(c, carry):
        carry = list(carry)
        for j in range(ct):
            kt = c * ct + j
            k = kt_ref[kt]
            v = v_ref[pl.ds(pl.multiple_of(kt * tq, tq), tq), :]
            vs = (jnp.where(lo_half, v, one_bf), jnp.where(lo_half, one_bf, v))
            bias = bias_scr[kt]
            for h in range(2):
                m, acc = carry[h]
                s = _dot(qs[h], k) + bias
                m_new = jnp.maximum(m, jnp.max(s, axis=-1, keepdims=True))
                p = jnp.exp(s - m_new).astype(BF16)
                carry[h] = (m_new, jnp.exp(m - m_new) * acc + _dot(p, vs[h]))
        return tuple(carry)

    one = (jnp.full((tq, 1), NEG, F32), jnp.zeros((tq, LANES), F32))
    (_, a0), (_, a1) = lax.fori_loop(0, n_chunks, chunk, (one, one))
    half = HEAD_DIM
    o_ref[...] = jnp.where(lo_half, a0 / pltpu.roll(a0, half, 1),
                           a1 / pltpu.roll(a1, half, 1)).astype(o_ref.dtype)


def _strict_lower(n):
    r = lax.broadcasted_iota(I32, (n, n), 0)
    c = lax.broadcasted_iota(I32, (n, n), 1)
    return (r > c).astype(BF16)


def _prompt_mixer(kind, q, kt, v, *, extra, name):
    b, _, t, _ = q.shape
    tq = _tiles(t)["attn"]
    ct = _tiles(t)["chunk"]
    nq = t // tq
    assert nq % ct == 0
    out_shape = jax.ShapeDtypeStruct((b, t, TOK_WIDTH), BF16)
    if kind == 0:
        iq, ikt, iw, n_sel = extra
        lmat = _strict_lower(tq).T
        kern = functools.partial(_attn_a_kernel, tq=tq, ct=ct, n_sel=float(n_sel))
        return pl.pallas_call(
            kern,
            grid=(b, nq, N_PAIRS),
            in_specs=[
                pl.BlockSpec((tq, tq), lambda bi, i, p: (0, 0)),
                pl.BlockSpec((None, tq, IDX_Q_WIDTH), lambda bi, i, p: (bi, i, 0)),
                pl.BlockSpec((None, nq, LANES, tq), lambda bi, i, p: (bi, 0, 0, 0)),
                pl.BlockSpec((None, tq, LANES), lambda bi, i, p: (bi, i, 0)),
                pl.BlockSpec((None, None, tq, LANES), lambda bi, i, p: (bi, p, i, 0)),
                pl.BlockSpec((None, nq, LANES, tq), lambda bi, i, p: (bi, 0, p, 0)),
                pl.BlockSpec((None, None, t, LANES), lambda bi, i, p: (bi, p, 0, 0)),
            ],
            out_specs=pl.BlockSpec((None, tq, LANES), lambda bi, i, p: (bi, i, p)),
            out_shape=out_shape,
            scratch_shapes=[pltpu.VMEM((nq, tq, tq), I32), pltpu.VMEM((nq, tq, tq), F32),
                            pltpu.VMEM((nq, tq, tq), I16), pltpu.VMEM((nq, tq, tq), I16)],
            compiler_params=_cparams(("parallel", "parallel", "arbitrary"), 48),
            name=name,
        )(lmat, iq, ikt, iw, q, kt, v)
    qkv_specs = [
        pl.BlockSpec((None, None, tq, LANES), lambda bi, p, i: (bi, p, i, 0)),
        pl.BlockSpec((None, nq, LANES, tq), lambda bi, p, i: (bi, 0, p, 0)),
        pl.BlockSpec((None, None, t, LANES), lambda bi, p, i: (bi, p, 0, 0)),
    ]
    out_spec = pl.BlockSpec((None, tq, LANES), lambda bi, p, i: (bi, i, p))
    if kind == 1:
        kern = functools.partial(_attn_b_kernel, tq=tq)
        consts = [_strict_lower(tq)]
        const_specs = [pl.BlockSpec((tq, tq), lambda bi, p, i: (0, 0))]
    else:
        lam_p, norm_g, lam_init = extra
        kern = functools.partial(_attn_c_kernel, tq=tq, ct=ct, lam_init=lam_init)
        consts = [lam_p, norm_g.reshape(1, LANES)]
        const_specs = [pl.BlockSpec(lam_p.shape, lambda bi, p, i: (0, 0)),
                       pl.BlockSpec((1, LANES), lambda bi, p, i: (0, 0))]
    return pl.pallas_call(
        kern,
        grid=(b, N_PAIRS, nq),
        in_specs=const_specs + qkv_specs,
        out_specs=out_spec,
        out_shape=out_shape,
        compiler_params=_cparams(("parallel", "parallel", "parallel"), 32),
        name=name,
    )(*consts, q, kt, v)


def _dec_idx_kernel(pt_ref, iq_ref, w_ref, ikn_ref, l_ref, *rest, pages, groups, n_sel):
    page_refs = rest[:pages]
    bias_ref = rest[pages]
    key_scr = rest[pages + 1]
    ts = bias_ref.shape[1]
    per_slot = pages // groups
    width = per_slot * PAGE_SIZE
    iq = iq_ref[...]
    w = w_ref[...]

    def to_keys(logits):
        lg = jnp.maximum(logits, 0.0) * w
        s = lg[0:ts]
        for h in range(1, IDX_HEADS):
            s = s + lg[h * ts:(h + 1) * ts]
        return _sort_key(s)

    ikn = jnp.concatenate([ikn_ref[...], jnp.zeros((PAGE_SIZE - ts, IDX_DIM), F32)], axis=0)
    key = to_keys(_dot_nt(iq, ikn.astype(BF16)))
    t_idx = lax.broadcasted_iota(I32, (ts, PAGE_SIZE), 0)
    j_idx = lax.broadcasted_iota(I32, (ts, PAGE_SIZE), 1)
    key = jnp.where(j_idx <= t_idx, key, INT_MIN)
    key_scr[0] = jnp.concatenate([key, jnp.full((ts, width - PAGE_SIZE), INT_MIN, I32)], axis=1)
    for sl in range(groups):
        key_scr[sl + 1] = jnp.concatenate(
            [to_keys(_dot(iq, r[...].astype(BF16)))
             for r in page_refs[sl * per_slot:(sl + 1) * per_slot]], axis=1)

    def _select():
        def count(pred):
            acc = jnp.zeros((ts, width), F32)
            for sl in range(groups + 1):
                acc = acc + jnp.where(pred(key_scr[sl]), 1.0, 0.0)
            return jnp.sum(acc, axis=-1, keepdims=True)

        def search(it, cur):
            cand = cur + jnp.left_shift(jnp.int32(1), 31 - it)
            return jnp.where(count(lambda key: key >= cand) >= n_sel, cand, cur)

        cur = lax.fori_loop(0, 32, search, jnp.full((ts, 1), INT_MIN, I32))
        thr = jnp.maximum(cur, INT_MIN + 1)
        ties = jnp.max(count(lambda key: key >= thr)) > n_sel

        @pl.when(jnp.logical_not(ties))
        def _by_threshold():
            for sl in range(groups + 1):
                bias_ref[sl] = jnp.where(key_scr[sl] >= thr, 0.0, NEG)

        @pl.when(ties)
        def _by_rank():
            need = n_sel - count(lambda key: key > thr)
            lmat = l_ref[...]
            seen = jnp.zeros((ts, 1), F32)
            for sl in list(range(1, groups + 1)) + [0]:
                key = key_scr[sl]
                eq = key == thr
                eqf = jnp.where(eq, 1.0, 0.0)
                eq16 = jnp.concatenate([eqf, jnp.zeros_like(eqf)], axis=0).astype(BF16)
                rank = _dot(eq16, lmat)[0:ts] + seen
                sel = (key > thr) | (eq & (rank < need))
                bias_ref[sl] = jnp.where(sel, 0.0, NEG)
                seen = seen + jnp.sum(eqf, axis=-1, keepdims=True)

    _select()


def _dec_attn_kernel(pt_ref, *refs, mode, pages, groups, ts, lam_init):
    it = iter(refs)
    qbd_ref, kn_ref, vn_ref = next(it), next(it), next(it)
    bias_ref = next(it) if mode == "a" else None
    u_ref = next(it) if mode == "b" else None
    lam_ref, gn_ref = (next(it), next(it)) if mode == "c" else (None, None)
    kt_refs = [next(it) for _ in range(pages)]
    vt_refs = [next(it) for _ in range(pages)]
    o_ref = next(it)
    m_scr, l_scr, acc_scr = next(it), next(it), next(it)
    g = pl.program_id(1)
    rows = 2 * N_PAIRS * ts
    qbd = qbd_ref[...]

    def rows_of(x):
        return jnp.concatenate([x] * (2 * N_PAIRS), axis=0)

    def step(s, pvs, valid, u):
        if mode == "b":
            sp = _softplus(s)
            spm = sp if valid is None else jnp.where(valid, sp, 0.0)
            cs = _dot(spm.astype(BF16), u)
            w = jnp.exp(s - sp - (cs + m_scr[...]))
            if valid is not None:
                w = jnp.where(valid, w, 0.0)
            m_scr[...] += cs[:, 0:1] + spm[:, 0:1]
            p = w.astype(BF16)
            acc = acc_scr[...]
        else:
            if valid is not None:
                s = jnp.where(valid, s, NEG)
            m_old = m_scr[...]
            m_new = jnp.maximum(m_old, jnp.max(s, axis=-1, keepdims=True))
            a = jnp.exp(m_old - m_new)
            pf = jnp.exp(s - m_new)
            l_scr[...] = a * l_scr[...] + jnp.sum(pf, axis=-1, keepdims=True)
            m_scr[...] = m_new
            p = pf.astype(BF16)
            acc = a * acc_scr[...]
        for j, pv in enumerate(pvs):
            acc = acc + pv(p[:, j * PAGE_SIZE:(j + 1) * PAGE_SIZE])
        acc_scr[...] = acc

    @pl.when(g == 0)
    def _new_keys():
        m_scr[...] = jnp.full((rows, 1), 0.0 if mode == "b" else NEG, F32)
        l_scr[...] = jnp.zeros((rows, 1), F32)
        acc_scr[...] = jnp.zeros((rows, TOK_WIDTH), F32)
        pad = jnp.zeros((PAGE_SIZE - ts, TOK_WIDTH), F32)
        kn = jnp.concatenate([kn_ref[...], pad], axis=0).astype(BF16)
        vn = jnp.concatenate([vn_ref[...], pad], axis=0).astype(BF16)
        s = _dot_nt(qbd, kn)
        t_idx = rows_of(lax.broadcasted_iota(I32, (ts, PAGE_SIZE), 0))
        j_idx = lax.broadcasted_iota(I32, (rows, PAGE_SIZE), 1)
        valid = (j_idx < t_idx) if mode == "b" else (j_idx <= t_idx)
        if mode == "a":
            s = s + rows_of(bias_ref[:, 0:PAGE_SIZE])
        step(s, [lambda p: _dot(p, vn)], valid,
             u_ref[0:PAGE_SIZE, 0:PAGE_SIZE] if mode == "b" else None)

    @pl.when(g > 0)
    def _pages():
        s = jnp.concatenate([_dot(qbd, r[...].astype(BF16)) for r in kt_refs], axis=1)
        if mode == "a":
            s = s + rows_of(bias_ref[...])
        pvs = [functools.partial(lambda p, r: _dot_nt(p, r[...].astype(BF16)), r=r) for r in vt_refs]
        step(s, pvs, None, u_ref[...] if mode == "b" else None)

    @pl.when(g == groups)
    def _finish():
        acc = acc_scr[...]
        if mode != "b":
            acc = acc / l_scr[...]
        lo_half = _lane_iota() < HEAD_DIM
        pieces = []
        for p in range(N_PAIRS):
            blk = acc[2 * p * ts:(2 * p + 2) * ts, p * LANES:(p + 1) * LANES]
            first, second = blk[0:ts], blk[ts:2 * ts]
            if mode == "c":
                o = first - _diff_lambda(lam_ref[...], lam_init) * second
                o = o * lax.rsqrt(jnp.mean(o * o, axis=-1, keepdims=True) + RMS_EPS)
                pieces.append(o * gn_ref[...] * (1.0 - lam_init))
            else:
                pieces.append(jnp.where(lo_half, first, second))
        o_ref[...] = jnp.concatenate(pieces, axis=1)


def _pages_per_step(n_pages):
    p = 8
    while n_pages % p:
        p //= 2
    return p


def _dec_indexer(page_table, cache_ikt, slot, iq_rows, w_rows, ik_new, n_sel, name):
    bs, n_pages = page_table.shape
    ts = ik_new.shape[1]
    groups = n_pages // _pages_per_step(n_pages)
    width = _pages_per_step(n_pages) * PAGE_SIZE
    lmat = _strict_lower(width).T

    def page_spec(p):
        return pl.BlockSpec((None, None, IDX_DIM, PAGE_SIZE),
                            lambda b, pt: (slot, pt[b, p], 0, 0))

    grid_spec = pltpu.PrefetchScalarGridSpec(
        num_scalar_prefetch=1,
        grid=(bs,),
        in_specs=[
            pl.BlockSpec((None, IDX_HEADS * ts, IDX_DIM), lambda b, pt: (b, 0, 0)),
            pl.BlockSpec((None, IDX_HEADS * ts, 1), lambda b, pt: (b, 0, 0)),
            pl.BlockSpec((None, ts, IDX_DIM), lambda b, pt: (b, 0, 0)),
            pl.BlockSpec((width, width), lambda b, pt: (0, 0)),
        ] + [page_spec(p) for p in range(n_pages)],
        out_specs=pl.BlockSpec((None, groups + 1, ts, width), lambda b, pt: (b, 0, 0, 0)),
        scratch_shapes=[pltpu.VMEM((groups + 1, ts, width), I32)],
    )
    kern = functools.partial(_dec_idx_kernel, pages=n_pages, groups=groups, n_sel=float(n_sel))
    return pl.pallas_call(
        kern,
        grid_spec=grid_spec,
        out_shape=jax.ShapeDtypeStruct((bs, groups + 1, ts, width), F32),
        compiler_params=_cparams(("parallel",), 32),
        name=name,
    )(page_table, iq_rows, w_rows, ik_new, lmat, *([cache_ikt] * n_pages))


def _dec_mixer(mode, layer, page_table, cache_kt, cache_vt, qbd, k_new, v_new, *, extra, name):
    bs, n_pages = page_table.shape
    ts = k_new.shape[1]
    pages = _pages_per_step(n_pages)
    groups = n_pages // pages
    width = pages * PAGE_SIZE
    rows = 2 * N_PAIRS * ts
    descending = mode == "b"

    def page_spec(p):
        def imap(b, g, pt):
            grp = jnp.maximum(g - 1, 0)
            if descending:
                grp = groups - 1 - grp
            return (layer, pt[b, grp * pages + p], 0, 0)
        return pl.BlockSpec((None, None, TOK_WIDTH, PAGE_SIZE), imap)

    in_specs = [
        pl.BlockSpec((None, rows, TOK_WIDTH), lambda b, g, pt: (b, 0, 0)),
        pl.BlockSpec((None, ts, TOK_WIDTH), lambda b, g, pt: (b, 0, 0)),
        pl.BlockSpec((None, ts, TOK_WIDTH), lambda b, g, pt: (b, 0, 0)),
    ]
    args = [qbd, k_new, v_new]
    lam_init = 0.0
    if mode == "a":
        in_specs.append(pl.BlockSpec((None, None, ts, width), lambda b, g, pt: (b, g, 0, 0)))
        args.append(extra)
    elif mode == "b":
        in_specs.append(pl.BlockSpec((width, width), lambda b, g, pt: (0, 0)))
        args.append(_strict_lower(width))
    else:
        lam_p, norm_g, lam_init = extra
        in_specs += [pl.BlockSpec(lam_p.shape, lambda b, g, pt: (0, 0)),
                     pl.BlockSpec((1, LANES), lambda b, g, pt: (0, 0))]
        args += [lam_p, norm_g.reshape(1, LANES)]
    in_specs += [page_spec(p) for p in range(pages)] * 2
    args += [cache_kt] * pages + [cache_vt] * pages
    grid_spec = pltpu.PrefetchScalarGridSpec(
        num_scalar_prefetch=1,
        grid=(bs, groups + 1),
        in_specs=in_specs,
        out_specs=pl.BlockSpec((None, ts, TOK_WIDTH), lambda b, g, pt: (b, 0, 0)),
        scratch_shapes=[pltpu.VMEM((rows, 1), F32), pltpu.VMEM((rows, 1), F32),
                        pltpu.VMEM((rows, TOK_WIDTH), F32)],
    )
    kern = functools.partial(_dec_attn_kernel, mode=mode, pages=pages, groups=groups, ts=ts,
                             lam_init=lam_init)
    return pl.pallas_call(
        kern,
        grid_spec=grid_spec,
        out_shape=jax.ShapeDtypeStruct((bs, ts, TOK_WIDTH), F32),
        compiler_params=_cparams(("parallel", "arbitrary"), 48),
        name=name,
    )(page_table, *args)


def _oproj_ln_kernel(tok_ref, mem_ref, x_ref, wt_ref, wm_ref, g_ref, b_ref, o_ref, obf_ref, *, alpha):
    mix = (_dot(tok_ref[...].astype(BF16), wt_ref[...])
           + _dot(mem_ref[...].astype(BF16), wm_ref[...]))
    out = _layer_norm(alpha * x_ref[...] + mix, g_ref[...], b_ref[...])
    o_ref[...] = out
    obf_ref[...] = out.astype(BF16)


def _oproj_ln(tok, mem, x, w_tok, w_mem, g, b, alpha, name):
    m_rows, d = x.shape
    tm = _tiles(m_rows)["proj"]
    row = lambda n: pl.BlockSpec((tm, n), lambda i: (i, 0))
    full = lambda a: pl.BlockSpec(a.shape, lambda i: (0, 0))
    return pl.pallas_call(
        functools.partial(_oproj_ln_kernel, alpha=alpha),
        grid=(m_rows // tm,),
        in_specs=[row(TOK_WIDTH), row(MEM_WIDTH), row(d), full(w_tok), full(w_mem), full(g), full(b)],
        out_specs=[row(d), row(d)],
        out_shape=[jax.ShapeDtypeStruct((m_rows, d), F32), jax.ShapeDtypeStruct((m_rows, d), BF16)],
        compiler_params=_cparams(("parallel",), 40),
        name=name,
    )(tok, mem, x, w_tok, w_mem, g, b)


def _ffn_ln_kernel(xbf_ref, x_ref, wg_ref, wu_ref, wd_ref, g_ref, b_ref, o_ref, obf_ref, acc_ref,
                   *, alpha, n_f):
    f = pl.program_id(1)
    xb = xbf_ref[...]
    hg = _dot(xb, wg_ref[...])
    hu = _dot(xb, wu_ref[...])
    act = hg * (1.0 / (1.0 + jnp.exp(-hg))) * hu
    part = _dot(act.astype(BF16), wd_ref[...])

    @pl.when(f == 0)
    def _first():
        acc_ref[...] = part

    @pl.when(f > 0)
    def _rest():
        acc_ref[...] += part

    @pl.when(f == n_f - 1)
    def _finish():
        out = _layer_norm(alpha * x_ref[...] + acc_ref[...], g_ref[...], b_ref[...])
        o_ref[...] = out
        obf_ref[...] = out.astype(BF16)


def _ffn_ln(xbf, x, w_gu, w_d, g, b, alpha, name):
    m_rows, d = x.shape
    d_ff = w_d.shape[0]
    tm = _tiles(m_rows)["ffn"]
    n_f = 2
    tf = d_ff // n_f
    row = lambda: pl.BlockSpec((tm, d), lambda i, f: (i, 0))
    vec = lambda a: pl.BlockSpec(a.shape, lambda i, f: (0, 0))
    return pl.pallas_call(
        functools.partial(_ffn_ln_kernel, alpha=alpha, n_f=n_f),
        grid=(m_rows // tm, n_f),
        in_specs=[row(), row(),
                  pl.BlockSpec((d, tf), lambda i, f: (0, f)),
                  pl.BlockSpec((d, tf), lambda i, f: (0, f + n_f)),
                  pl.BlockSpec((tf, d), lambda i, f: (f, 0)),
                  vec(g), vec(b)],
        out_specs=[row(), row()],
        out_shape=[jax.ShapeDtypeStruct((m_rows, d), F32), jax.ShapeDtypeStruct((m_rows, d), BF16)],
        scratch_shapes=[pltpu.VMEM((tm, d), F32)],
        compiler_params=_cparams(("parallel", "arbitrary"), 56),
        name=name,
    )(xbf, x, w_gu, w_gu, w_d, g, b)


def _rope_tables(pos):
    half = HEAD_DIM // 2
    inv = ROPE_THETA ** (-jnp.arange(half, dtype=F32) / half)
    ang = pos.astype(F32)[:, None] * inv[None, :]
    cos = jnp.cos(ang)
    sin = jnp.sin(ang)
    return jnp.tile(cos, (1, 4)), jnp.tile(jnp.concatenate([-sin, sin], axis=1), (1, 2))


def _heads_last(xt, n_heads):
    l, b, _, p = xt.shape
    return xt.reshape(l, b, n_heads, HEAD_DIM, p).transpose(0, 1, 4, 2, 3)


def kernel(x_prompt, x_sample, mem_prompt, cache_k, cache_v, cache_idx_k, cache_mem_k, cache_mem_v, page_table, w_in, w_idx, w_mem_kv, w_o, ln_mix_g, ln_mix_b, w_gate_up, w_down, ln_ffn_g, ln_ffn_b, diff_lambda, diff_norm_g):
    bp, tp, d_model = x_prompt.shape
    bs, ts, _ = x_sample.shape
    n_pages = page_table.shape[1]
    past = n_pages * PAGE_SIZE
    depth = w_in.shape[0]
    n_mem = mem_prompt.shape[1]
    n_pool = cache_k.shape[1]
    n_heads = 2 * N_PAIRS
    mem_heads = MEM_WIDTH // HEAD_DIM
    alpha = (2 * depth) ** 0.25
    nsel_p = min(TOPK_MAX, tp // 4)
    nsel_s = min(TOPK_MAX, (past + ts) // 4)
    mp, ms = bp * tp, bs * ts

    w_in_b = w_in.astype(BF16)
    w_mkv_b = w_mem_kv.astype(BF16)
    w_o_b = w_o.astype(BF16)
    w_gu_b = w_gate_up.astype(BF16)
    w_d_b = w_down.astype(BF16)
    iq_w, ik_w, ih_w = (w_idx[..., :IDX_Q_WIDTH], w_idx[..., IDX_Q_WIDTH:IDX_Q_WIDTH + IDX_DIM],
                        w_idx[..., IDX_Q_WIDTH + IDX_DIM:])
    w_idx_b = jnp.concatenate(
        [iq_w, ik_w, ik_w, ih_w, jnp.zeros(ih_w.shape[:2] + (LANES - IDX_HEADS,), F32)],
        axis=-1).astype(BF16)
    idx_cols = IDX_Q_WIDTH + 2 * LANES

    rope_p = _rope_tables(jnp.arange(tp))
    rope_s = _rope_tables(jnp.tile(past + jnp.arange(ts), bs))
    cache_kt = cache_k.transpose(0, 1, 3, 4, 2).reshape(depth, n_pool, TOK_WIDTH, PAGE_SIZE)
    cache_vt = cache_v.transpose(0, 1, 3, 4, 2).reshape(depth, n_pool, TOK_WIDTH, PAGE_SIZE)
    cache_ikt = cache_idx_k.transpose(0, 1, 3, 2)
    cache_mkt = cache_mem_k.transpose(0, 1, 3, 4, 2).reshape(depth, bs, MEM_WIDTH, n_mem)
    cache_mvt = cache_mem_v.transpose(0, 1, 3, 4, 2).reshape(depth, bs, MEM_WIDTH, n_mem)
    mem2d = mem_prompt.reshape(bp * n_mem, d_model)
    head_of_col = jnp.arange(TOK_WIDTH) // HEAD_DIM
    bd_mask = (jnp.arange(n_heads)[:, None, None] == head_of_col[None, None, :])

    xp = x_prompt.reshape(mp, d_model)
    xs = x_sample.reshape(ms, d_model)
    xp_in, xs_in = xp, xs
    nk_p, nv_p, nidx_p, nmk_p, nmv_p, nk_s, nv_s, nidx_s = [], [], [], [], [], [], [], []

    for l in range(depth):
        kind = l % N_MIXERS
        slot = l // N_MIXERS
        n_rope = n_heads if kind != 1 else 0
        tag = f"l{l}"
        c_q, c_k, c_v, c_m = 0, TOK_WIDTH, 2 * TOK_WIDTH, 3 * TOK_WIDTH
        in_w = c_m + MEM_WIDTH

        q_p, kt_f, kt_p, vt_f, v_p, mq_p = _proj(
            xp_in, w_in_b[l],
            (("pair", c_q, c_k, QK_SCALE, BF16), ("T", c_k, c_v, 1.0, F32),
             ("Ttile", c_k, c_v, 1.0, BF16), ("T", c_v, c_m, 1.0, F32),
             ("pair", c_v, c_m, 1.0, BF16), ("flat", c_m, in_w, QK_SCALE, BF16)),
            batch=bp, rope=rope_p, n_rope_chunks=n_rope, name=f"proj_p_{tag}")
        nk_p.append(kt_f)
        nv_p.append(vt_f)
        mkt_f, mvt_f = _proj(
            mem2d, w_mkv_b[l],
            (("T", 0, MEM_WIDTH, 1.0, F32), ("T", MEM_WIDTH, 2 * MEM_WIDTH, 1.0, F32)),
            batch=bp, name=f"proj_mem_{tag}")
        nmk_p.append(mkt_f)
        nmv_p.append(mvt_f)
        mem_p = _mem_attn(mq_p.reshape(bp, tp, MEM_WIDTH), mkt_f, mvt_f, None, BF16, f"mem_p_{tag}")

        q_s, k_s, v_s, mq_s = _proj(
            xs_in, w_in_b[l],
            (("flat", c_q, c_k, QK_SCALE, F32), ("flat", c_k, c_v, 1.0, F32),
             ("flat", c_v, c_m, 1.0, F32), ("flat", c_m, in_w, QK_SCALE, F32)),
            batch=bs, rope=rope_s, n_rope_chunks=n_rope, name=f"proj_s_{tag}")
        k_s3 = k_s.reshape(bs, ts, TOK_WIDTH)
        v_s3 = v_s.reshape(bs, ts, TOK_WIDTH)
        nk_s.append(k_s3.reshape(bs, ts, n_heads, HEAD_DIM))
        nv_s.append(v_s3.reshape(bs, ts, n_heads, HEAD_DIM))
        qbd = jnp.where(bd_mask[None], q_s.reshape(bs, 1, ts, TOK_WIDTH), 0.0)
        qbd = qbd.reshape(bs, n_heads * ts, TOK_WIDTH).astype(BF16)
        mem_s = _mem_attn(mq_s.reshape(bs, ts, MEM_WIDTH), cache_mkt, cache_mvt, l, F32,
                          f"mem_s_{tag}")

        if kind == 0:
            iq_p, ikt_p, ikt_f, iw_p = _proj(
                xp_in, w_idx_b[slot],
                (("flat", 0, IDX_Q_WIDTH, 1.0, BF16),
                 ("Ttile", IDX_Q_WIDTH, IDX_Q_WIDTH + LANES, 1.0, BF16),
                 ("T", IDX_Q_WIDTH, IDX_Q_WIDTH + IDX_DIM, 1.0, F32),
                 ("flat", IDX_Q_WIDTH + LANES, idx_cols, IDX_W_SCALE, F32)),
                batch=bp, rope=rope_p, n_rope_chunks=IDX_Q_WIDTH // LANES + 1, name=f"proj_ip_{tag}")
            nidx_p.append(ikt_f)
            tok_p = _prompt_mixer(
                0, q_p, kt_p, v_p,
                extra=(iq_p.reshape(bp, tp, IDX_Q_WIDTH), ikt_p, iw_p.reshape(bp, tp, LANES), nsel_p),
                name=f"mix_a_{tag}")

            iq_s, ik_s, iw_s = _proj(
                xs_in, w_idx_b[slot],
                (("flat", 0, IDX_Q_WIDTH, 1.0, F32),
                 ("flat", IDX_Q_WIDTH, IDX_Q_WIDTH + IDX_DIM, 1.0, F32),
                 ("flat", IDX_Q_WIDTH + LANES, idx_cols, IDX_W_SCALE, F32)),
                batch=bs, rope=rope_s, n_rope_chunks=IDX_Q_WIDTH // LANES + 1, name=f"proj_is_{tag}")
            ik_s3 = ik_s.reshape(bs, ts, IDX_DIM)
            nidx_s.append(ik_s3)
            iq_rows = iq_s.reshape(bs, ts, IDX_HEADS, IDX_DIM).swapaxes(1, 2)
            iq_rows = iq_rows.reshape(bs, IDX_HEADS * ts, IDX_DIM).astype(BF16)
            w_rows = iw_s.reshape(bs, ts, LANES)[:, :, :IDX_HEADS].swapaxes(1, 2)
            w_rows = w_rows.reshape(bs, IDX_HEADS * ts, 1)
            bias = _dec_indexer(page_table, cache_ikt, slot, iq_rows, w_rows, ik_s3, nsel_s,
                                f"idx_s_{tag}")
            tok_s = _dec_mixer("a", l, page_table, cache_kt, cache_vt, qbd, k_s3, v_s3,
                               extra=bias, name=f"mix_sa_{tag}")
        elif kind == 1:
            tok_p = _prompt_mixer(1, q_p, kt_p, v_p, extra=None, name=f"mix_b_{tag}")
            tok_s = _dec_mixer("b", l, page_table, cache_kt, cache_vt, qbd, k_s3, v_s3,
                               extra=None, name=f"mix_sb_{tag}")
        else:
            lam_init = 0.8 - 0.6 * math.exp(-0.3 * l)
            extra = (diff_lambda[slot], diff_norm_g[slot], lam_init)
            tok_p = _prompt_mixer(2, q_p, kt_p, v_p, extra=extra, name=f"mix_c_{tag}")
            tok_s = _dec_mixer("c", l, page_table, cache_kt, cache_vt, qbd, k_s3, v_s3,
                               extra=extra, name=f"mix_sc_{tag}")

        w_tok, w_mem = w_o_b[l, :TOK_WIDTH], w_o_b[l, TOK_WIDTH:]
        g1, b1 = ln_mix_g[l].reshape(1, d_model), ln_mix_b[l].reshape(1, d_model)
        g2, b2 = ln_ffn_g[l].reshape(1, d_model), ln_ffn_b[l].reshape(1, d_model)
        x1, x1b = _oproj_ln(tok_p.reshape(mp, TOK_WIDTH), mem_p.reshape(mp, MEM_WIDTH), xp,
                            w_tok, w_mem, g1, b1, alpha, f"oproj_p_{tag}")
        xp, xp_in = _ffn_ln(x1b, x1, w_gu_b[l], w_d_b[l], g2, b2, alpha, f"ffn_p_{tag}")
        y1, y1b = _oproj_ln(tok_s.reshape(ms, TOK_WIDTH), mem_s.reshape(ms, MEM_WIDTH), xs,
                            w_tok, w_mem, g1, b1, alpha, f"oproj_s_{tag}")
        xs, xs_in = _ffn_ln(y1b, y1, w_gu_b[l], w_d_b[l], g2, b2, alpha, f"ffn_s_{tag}")

    return (xp.reshape(bp, tp, d_model), xs.reshape(bs, ts, d_model),
            _heads_last(jnp.stack(nk_p), n_heads), _heads_last(jnp.stack(nv_p), n_heads),
            jnp.stack(nidx_p).transpose(0, 1, 3, 2),
            _heads_last(jnp.stack(nmk_p), mem_heads), _heads_last(jnp.stack(nmv_p), mem_heads),
            jnp.stack(nk_s), jnp.stack(nv_s), jnp.stack(nidx_s))
```

```python
import functools
import math

import jax
import jax.numpy as jnp
from jax import lax
from jax.experimental import pallas as pl
from jax.experimental.pallas import tpu as pltpu

F32 = jnp.float32
BF16 = jnp.bfloat16
I32 = jnp.int32

HEAD_DIM = 64
TOK_WIDTH = 768
N_PAIRS = 6
MEM_WIDTH = 256
IDX_HEADS = 8
IDX_DIM = 64
IDX_Q_WIDTH = IDX_HEADS * IDX_DIM
IDX_W_SCALE = IDX_HEADS ** -0.5 * IDX_DIM ** -0.5
PAGE_SIZE = 128
TOPK_MAX = 256
N_MIXERS = 3
ROPE_THETA = 10000.0
LN_EPS = 1e-5
RMS_EPS = 1e-5
QK_SCALE = HEAD_DIM ** -0.5

LANES = 128
NEG = -1e30
STICK_EXIT = 110.0
INT_MIN = -2 ** 31
HALF16 = 2 ** 15
I16 = jnp.int16
MIB = 1024 * 1024


def _tiles(rows):
    attn = min(256, rows)
    return dict(proj=min(512, rows), ffn=min(512, rows), attn=attn, chunk=min(4, rows // attn))


def _cparams(sem, vmem_mib):
    return pltpu.CompilerParams(dimension_semantics=sem, vmem_limit_bytes=vmem_mib * MIB)


def _dot(a, b):
    return jnp.dot(a, b, preferred_element_type=F32)


def _dot_nt(a, b):
    return lax.dot_general(a, b, (((1,), (1,)), ((), ())), preferred_element_type=F32)


def _lane_iota():
    return lax.broadcasted_iota(I32, (1, LANES), 1)


def _layer_norm(y, g, b):
    mu = jnp.mean(y, axis=-1, keepdims=True)
    d = y - mu
    var = jnp.mean(d * d, axis=-1, keepdims=True)
    return d * lax.rsqrt(var + LN_EPS) * g + b


def _softmax_update(s, m, l, acc, pv):
    m_new = jnp.maximum(m, jnp.max(s, axis=-1, keepdims=True))
    a = jnp.exp(m - m_new)
    p = jnp.exp(s - m_new)
    l = a * l + jnp.sum(p, axis=-1, keepdims=True)
    acc = a * acc + pv(p.astype(BF16))
    return m_new, l, acc


def _sort_key(x):
    bits = lax.bitcast_convert_type(x, I32)
    return bits ^ ((bits >> 31) & 0x7FFFFFFF)


def _softplus(z):
    return jnp.maximum(z, 0.0) + jnp.log(1.0 + jnp.exp(-jnp.abs(z)))


def _diff_lambda(lp, lam_init):
    return (jnp.exp(jnp.sum(lp[0:1] * lp[1:2], axis=-1, keepdims=True))
            - jnp.exp(jnp.sum(lp[2:3] * lp[3:4], axis=-1, keepdims=True)) + lam_init)


def _head_split(q):
    lo_half = _lane_iota() < HEAD_DIM
    zero = jnp.zeros_like(q)
    return jnp.where(lo_half, q, zero), jnp.where(lo_half, zero, q)


def _proj_kernel(*refs, n_rope_chunks, outs, tk):
    x_ref, w_ref = refs[:2]
    if n_rope_chunks:
        cos_ref, sin_ref = refs[2:4]
    o_refs = refs[len(refs) - len(outs):]
    y = _dot(x_ref[...].astype(BF16), w_ref[...])
    n_chunks = y.shape[1] // LANES
    chunks = [y[:, c * LANES:(c + 1) * LANES] for c in range(n_chunks)]
    if n_rope_chunks:
        cos = cos_ref[...]
        sin = sin_ref[...]
        first = (_lane_iota() & (HEAD_DIM - 1)) < HEAD_DIM // 2
        for c in range(n_rope_chunks):
            yc = chunks[c]
            partner = jnp.where(first, pltpu.roll(yc, LANES - HEAD_DIM // 2, 1),
                                pltpu.roll(yc, HEAD_DIM // 2, 1))
            chunks[c] = yc * cos + partner * sin
    for o_ref, (kind, lo, hi, scale) in zip(o_refs, outs):
        c0, c1 = lo // LANES, -(-hi // LANES)
        if kind == "pair":
            for p in range(c1 - c0):
                v = chunks[c0 + p]
                o_ref[p] = (v * scale if scale != 1.0 else v).astype(o_ref.dtype)
            continue
        v = chunks[c0] if c1 - c0 == 1 else jnp.concatenate(chunks[c0:c1], axis=1)
        if scale != 1.0:
            v = v * scale
        if kind == "flat":
            o_ref[...] = v[:, lo - c0 * LANES:hi - c0 * LANES].astype(o_ref.dtype)
        elif kind == "T":
            o_ref[...] = v.T[0:hi - lo].astype(o_ref.dtype)
        else:
            vt = v.T.astype(o_ref.dtype)
            for j in range(vt.shape[1] // tk):
                o_ref[j] = vt[:, j * tk:(j + 1) * tk]


def _proj(x, w, outs, *, batch, rope=None, n_rope_chunks=0, name):
    m_rows, k_dim = x.shape
    n_cols = w.shape[1]
    t_len = m_rows // batch
    structured = any(o[0] != "flat" for o in outs)
    tm = _tiles(t_len if structured else m_rows)["proj"]
    tk = _tiles(t_len)["attn"]
    tiles_per_seq = max(t_len // tm, 1)
    in_specs = [pl.BlockSpec((tm, k_dim), lambda i: (i, 0)),
                pl.BlockSpec((k_dim, n_cols), lambda i: (0, 0))]
    args = [x, w]
    if n_rope_chunks:
        cos, sin = rope
        n_tab = cos.shape[0] // tm
        in_specs += [pl.BlockSpec((tm, LANES), lambda i: (i % n_tab, 0))] * 2
        args += [cos, sin]
    seq_of = lambda i: i // tiles_per_seq
    tile_of = lambda i: i % tiles_per_seq
    out_shapes, out_specs = [], []
    for kind, lo, hi, _, dtype in outs:
        if kind == "pair":
            groups = (hi - lo) // LANES
            shape, block = (batch, groups, t_len, LANES), (None, groups, tm, LANES)
            imap = lambda i: (seq_of(i), 0, tile_of(i), 0)
        elif kind == "T":
            shape, block = (batch, hi - lo, t_len), (None, hi - lo, tm)
            imap = lambda i: (seq_of(i), 0, tile_of(i))
        elif kind == "Ttile":
            shape, block = (batch, t_len // tk, hi - lo, tk), (None, tm // tk, hi - lo, tk)
            imap = lambda i: (seq_of(i), tile_of(i), 0, 0)
        else:
            shape, block = (m_rows, hi - lo), (tm, hi - lo)
            imap = lambda i: (i, 0)
        out_shapes.append(jax.ShapeDtypeStruct(shape, dtype))
        out_specs.append(pl.BlockSpec(block, imap))
    kern = functools.partial(_proj_kernel, n_rope_chunks=n_rope_chunks,
                             outs=tuple(o[:4] for o in outs), tk=tk)
    return pl.pallas_call(
        kern,
        grid=(m_rows // tm,),
        in_specs=in_specs,
        out_specs=out_specs,
        out_shape=out_shapes,
        compiler_params=_cparams(("parallel",), 56),
        name=name,
    )(*args)


def _mem_attn_kernel(q_ref, kt_ref, vt_ref, o_ref):
    lo_half = _lane_iota() < HEAD_DIM
    pieces = []
    for pr in range(MEM_WIDTH // LANES):
        sl = slice(pr * LANES, (pr + 1) * LANES)
        kt = kt_ref[sl, :].astype(BF16)
        vt = vt_ref[sl, :].astype(BF16)
        res = []
        for qc in _head_split(q_ref[:, sl].astype(BF16)):
            s = _dot(qc, kt)
            m = jnp.max(s, axis=-1, keepdims=True)
            p = jnp.exp(s - m)
            l = jnp.sum(p, axis=-1, keepdims=True)
            res.append(_dot_nt(p.astype(BF16), vt) / l)
        pieces.append(jnp.where(lo_half, res[0], res[1]))
    o_ref[...] = jnp.concatenate(pieces, axis=1).astype(o_ref.dtype)


def _mem_attn(q, kt, vt, layer, out_dtype, name):
    b, t, _ = q.shape
    tq = _tiles(t)["attn"]
    n_mem = kt.shape[-1]
    if layer is None:
        kv_spec = pl.BlockSpec((None, MEM_WIDTH, n_mem), lambda bi, i: (bi, 0, 0))
    else:
        kv_spec = pl.BlockSpec((None, None, MEM_WIDTH, n_mem), lambda bi, i: (layer, bi, 0, 0))
    return pl.pallas_call(
        _mem_attn_kernel,
        grid=(b, t // tq),
        in_specs=[pl.BlockSpec((None, tq, MEM_WIDTH), lambda bi, i: (bi, i, 0)), kv_spec, kv_spec],
        out_specs=pl.BlockSpec((None, tq, MEM_WIDTH), lambda bi, i: (bi, i, 0)),
        out_shape=jax.ShapeDtypeStruct((b, t, MEM_WIDTH), out_dtype),
        compiler_params=_cparams(("parallel", "parallel"), 32),
        name=name,
    )(q, kt, vt)


def _tile_iotas(t):
    return (lax.broadcasted_iota(I32, (t, t), 0), lax.broadcasted_iota(I32, (t, t), 1))


def _attn_c_kernel(lam_ref, g_ref, q_ref, kt_ref, v_ref, o_ref, *, tq, ct, lam_init):
    i = pl.program_id(2)
    diag_chunk = i // ct
    qs = _head_split(q_ref[...])

    row, col = _tile_iotas(tq)
    one_bf = jnp.ones((tq, LANES), BF16)

    def chunk(c, carry, diag):
        carry = list(carry)
        for j in range(ct):
            kt = c * ct + j
            k = kt_ref[kt]
            v = v_ref[pl.ds(pl.multiple_of(kt * tq, tq), tq), :]
            v1 = jnp.concatenate([v, one_bf], axis=1)
            for h in range(2):
                m, acc = carry[h]
                s = _dot(qs[h], k)
                if diag:
                    s = jnp.where(col + kt * tq <= row + i * tq, s, NEG)
                m_new = jnp.maximum(m, jnp.max(s, axis=-1, keepdims=True))
                p = jnp.exp(s - m_new).astype(BF16)
                carry[h] = (m_new, jnp.exp(m - m_new) * acc + _dot(p, v1))
        return tuple(carry)

    one = (jnp.full((tq, 1), NEG, F32), jnp.zeros((tq, 2 * LANES), F32))
    carry = lax.fori_loop(0, diag_chunk, lambda c, cr: chunk(c, cr, False), (one, one))
    (_, acc0), (_, acc1) = chunk(diag_chunk, carry, True)
    o = (acc0[:, :LANES] / acc0[:, LANES:]
         - _diff_lambda(lam_ref[...], lam_init) * (acc1[:, :LANES] / acc1[:, LANES:]))
    o = o * lax.rsqrt(jnp.mean(o * o, axis=-1, keepdims=True) + RMS_EPS)
    o_ref[...] = (o * g_ref[...] * (1.0 - lam_init)).astype(o_ref.dtype)


def _attn_b_kernel(u_ref, q_ref, kt_ref, v_ref, o_ref, *, tq):
    i = pl.program_id(2)
    qs = _head_split(q_ref[...])
    u = u_ref[...]
    row, col = _tile_iotas(tq)
    strict = col < row

    def tile(kt, carry, diag):
        k = kt_ref[kt]
        v = v_ref[pl.ds(pl.multiple_of(kt * tq, tq), tq), :]
        new = []
        for c in range(2):
            later, acc = carry[c]
            z = _dot(qs[c], k)
            sp = _softplus(z)
            spm = jnp.where(strict, sp, 0.0) if diag else sp
            cs = _dot(spm.astype(BF16), u)
            w = jnp.exp(z - sp - (cs + later))
            if diag:
                w = jnp.where(strict, w, 0.0)
            acc = acc + _dot(w.astype(BF16), v)
            later = later + cs[:, 0:1] + spm[:, 0:1]
            new.append((later, acc))
        return tuple(new)

    def live(carry):
        return (jnp.minimum(jnp.min(carry[0][0]), jnp.min(carry[1][0])) < STICK_EXIT).astype(I32)

    def cond(state):
        return (state[0] >= 0) & (state[1] > 0)

    def body(state):
        carry = tile(state[0], state[2], False)
        return state[0] - 1, live(carry), carry

    one = (jnp.zeros((tq, 1), F32), jnp.zeros((tq, LANES), F32))
    carry = tile(i, (one, one), True)
    carry = lax.while_loop(cond, body, (i - 1, live(carry), carry))[2]
    lo_half = _lane_iota() < HEAD_DIM
    o_ref[...] = jnp.where(lo_half, carry[0][1], carry[1][1]).astype(o_ref.dtype)


def _fold_lanes(c):
    out = c[:, 0:LANES]
    for j in range(1, c.shape[1] // LANES):
        out = out + c[:, j * LANES:(j + 1) * LANES]
    return out


def _attn_a_kernel(l_ref, iq_ref, ikt_ref, iw_ref, q_ref, kt_ref, v_ref, o_ref, key_scr, bias_scr,
                   hi_scr, lo_scr, *, tq, ct, n_sel):
    i = pl.program_id(1)
    pair = pl.program_id(2)
    n_chunks = i // ct + 1
    row, col = _tile_iotas(tq)
    causal = col <= row

    @pl.when(pair == 0)
    def _select():
        w_all = iw_ref[...]
        iq_heads = []
        for pr in range(IDX_Q_WIDTH // LANES):
            iq_heads += list(_head_split(iq_ref[:, pr * LANES:(pr + 1) * LANES]))

        def score_tile(kt, diag):
            ik = ikt_ref[kt]
            sc = jnp.zeros((tq, tq), F32)
            for h in range(IDX_HEADS):
                sc = sc + w_all[:, h:h + 1] * jnp.maximum(_dot(iq_heads[h], ik), 0.0)
            key = _sort_key(sc)
            if diag:
                key = jnp.where(causal, key, INT_MIN)
            key_scr[kt] = key
            hi_scr[kt] = (key >> 16).astype(I16)
            lo_scr[kt] = ((key & 0xFFFF) - HALF16).astype(I16)

        def _score_body(kt, c):
            score_tile(kt, False)
            return c

        lax.fori_loop(0, i, _score_body, 0)
        score_tile(i, True)

        def _pad_body(kt, c):
            bias_scr[kt] = jnp.full((tq, tq), NEG, F32)
            return c

        lax.fori_loop(i + 1, n_chunks * ct, _pad_body, 0)

        def count(pred):
            def body(kt, acc):
                return acc + _fold_lanes(jnp.where(pred(key_scr[kt]), 1.0, 0.0))
            acc = lax.fori_loop(0, i + 1, body, jnp.zeros((tq, LANES), F32))
            return jnp.sum(acc, axis=-1, keepdims=True)

        one16, zero16 = jnp.int16(1), jnp.int16(0)

        def count16(scr, bound, strict):
            b16 = jnp.broadcast_to(bound, (tq, tq)).astype(I16)

            def body(kt, acc):
                hit = (scr[kt] > b16) if strict else (scr[kt] >= b16)
                return acc + jnp.where(hit, one16, zero16)

            acc = lax.fori_loop(0, i + 1, body, jnp.zeros((tq, tq), I16))
            return jnp.sum(_fold_lanes(acc.astype(F32)), axis=-1, keepdims=True)

        def kth_largest16(scr, k):
            def step(it, cur):
                cand = cur + jnp.left_shift(jnp.int32(1), 15 - it)
                return jnp.where(count16(scr, cand, False) >= k, cand, cur)
            return lax.fori_loop(0, 16, step, jnp.full((tq, 1), -HALF16, I32))

        hi_thr = kth_largest16(hi_scr, n_sel)
        rank_lo = n_sel - count16(hi_scr, hi_thr, True)
        hi_thr16 = jnp.broadcast_to(hi_thr, (tq, tq)).astype(I16)

        def _keep_body(kt, c):
            lo_scr[kt] = jnp.where(hi_scr[kt] == hi_thr16, lo_scr[kt], jnp.int16(-HALF16))
            return c

        lax.fori_loop(0, i + 1, _keep_body, 0)
        lo_thr = kth_largest16(lo_scr, rank_lo)
        cur = jnp.left_shift(hi_thr, 16) + (lo_thr + HALF16)
        thr = jnp.maximum(cur, INT_MIN + 1)
        ties = jnp.max(count(lambda key: key >= thr)) > n_sel

        @pl.when(jnp.logical_not(ties))
        def _by_threshold():
            def body(kt, c):
                bias_scr[kt] = jnp.where(key_scr[kt] >= thr, 0.0, NEG)
                return c
            lax.fori_loop(0, i + 1, body, 0)

        @pl.when(ties)
        def _by_rank():
            need = n_sel - count(lambda key: key > thr)
            lmat = l_ref[...]

            def body(kt, seen):
                key = key_scr[kt]
                eq = key == thr
                eqf = jnp.where(eq, 1.0, 0.0)
                rank = _dot(eqf.astype(BF16), lmat) + seen
                sel = (key > thr) | (eq & (rank < need))
                bias_scr[kt] = jnp.where(sel, 0.0, NEG)
                return seen + jnp.sum(eqf, axis=-1, keepdims=True)

            lax.fori_loop(0, i + 1, body, jnp.zeros((tq, 1), F32))

    qs = _head_split(q_ref[...])
    lo_half = _lane_iota() < HEAD_DIM
    one_bf = jnp.ones((tq, LANES), BF16)

    def chunk(c, carry):
        carry = list(carry)
        for j in range(ct):
            kt = c * ct + j
            k = kt_ref[kt]
            v = v_ref[pl.ds(pl.multiple_of(kt * tq, tq), tq), :]
            vs = (jnp.where(lo_half, v, one_bf), jnp.where(lo_half, one_bf, v))
            bias = bias_scr[kt]
            for h in range(2):
                m, acc = carry[h]
                s = _dot(qs[h], k) + bias
                m_new = jnp.maximum(m, jnp.max(s, axis=-1, keepdims=True))
                p = jnp.exp(s - m_new).astype(BF16)
                carry[h] = (m_new, jnp.exp(m - m_new) * acc + _dot(p, vs[h]))
        return tuple(carry)

    one = (jnp.full((tq, 1), NEG, F32), jnp.zeros((tq, LANES), F32))
    (_, a0), (_, a1) = lax.fori_loop(0, n_chunks, chunk, (one, one))
    half = HEAD_DIM
    o_ref[...] = jnp.where(lo_half, a0 / pltpu.roll(a0, half, 1),
                           a1 / pltpu.roll(a1, half, 1)).astype(o_ref.dtype)


def _strict_lower(n):
    r = lax.broadcasted_iota(I32, (n, n), 0)
    c = lax.broadcasted_iota(I32, (n, n), 1)
    return (r > c).astype(BF16)


def _prompt_mixer(kind, q, kt, v, *, extra, name):
    b, _, t, _ = q.shape
    tq = _tiles(t)["attn"]
    ct = _tiles(t)["chunk"]
    nq = t // tq
    assert nq % ct == 0
    out_shape = jax.ShapeDtypeStruct((b, t, TOK_WIDTH), BF16)
    if kind == 0:
        iq, ikt, iw, n_sel = extra
        lmat = _strict_lower(tq).T
        kern = functools.partial(_attn_a_kernel, tq=tq, ct=ct, n_sel=float(n_sel))
        return pl.pallas_call(
            kern,
            grid=(b, nq, N_PAIRS),
            in_specs=[
                pl.BlockSpec((tq, tq), lambda bi, i, p: (0, 0)),
                pl.BlockSpec((None, tq, IDX_Q_WIDTH), lambda bi, i, p: (bi, i, 0)),
                pl.BlockSpec((None, nq, LANES, tq), lambda bi, i, p: (bi, 0, 0, 0)),
                pl.BlockSpec((None, tq, LANES), lambda bi, i, p: (bi, i, 0)),
                pl.BlockSpec((None, None, tq, LANES), lambda bi, i, p: (bi, p, i, 0)),
                pl.BlockSpec((None, nq, LANES, tq), lambda bi, i, p: (bi, 0, p, 0)),
                pl.BlockSpec((None, None, t, LANES), lambda bi, i, p: (bi, p, 0, 0)),
            ],
            out_specs=pl.BlockSpec((None, tq, LANES), lambda bi, i, p: (bi, i, p)),
            out_shape=out_shape,
            scratch_shapes=[pltpu.VMEM((nq, tq, tq), I32), pltpu.VMEM((nq, tq, tq), F32),
                            pltpu.VMEM((nq, tq, tq), I16), pltpu.VMEM((nq, tq, tq), I16)],
            compiler_params=_cparams(("parallel", "parallel", "arbitrary"), 48),
            name=name,
        )(lmat, iq, ikt, iw, q, kt, v)
    qkv_specs = [
        pl.BlockSpec((None, None, tq, LANES), lambda bi, p, i: (bi, p, i, 0)),
        pl.BlockSpec((None, nq, LANES, tq), lambda bi, p, i: (bi, 0, p, 0)),
        pl.BlockSpec((None, None, t, LANES), lambda bi, p, i: (bi, p, 0, 0)),
    ]
    out_spec = pl.BlockSpec((None, tq, LANES), lambda bi, p, i: (bi, i, p))
    if kind == 1:
        kern = functools.partial(_attn_b_kernel, tq=tq)
        consts = [_strict_lower(tq)]
        const_specs = [pl.BlockSpec((tq, tq), lambda bi, p, i: (0, 0))]
    else:
        lam_p, norm_g, lam_init = extra
        kern = functools.partial(_attn_c_kernel, tq=tq, ct=ct, lam_init=lam_init)
        consts = [lam_p, norm_g.reshape(1, LANES)]
        const_specs = [pl.BlockSpec(lam_p.shape, lambda bi, p, i: (0, 0)),
                       pl.BlockSpec((1, LANES), lambda bi, p, i: (0, 0))]
    return pl.pallas_call(
        kern,
        grid=(b, N_PAIRS, nq),
        in_specs=const_specs + qkv_specs,
        out_specs=out_spec,
        out_shape=out_shape,
        compiler_params=_cparams(("parallel", "parallel", "parallel"), 32),
        name=name,
    )(*consts, q, kt, v)


def _dec_idx_kernel(pt_ref, iq_ref, w_ref, ikn_ref, l_ref, *rest, pages, groups, n_sel):
    page_refs = rest[:pages]
    bias_ref = rest[pages]
    key_scr = rest[pages + 1]
    ts = bias_ref.shape[1]
    per_slot = pages // groups
    width = per_slot * PAGE_SIZE
    iq = iq_ref[...]
    w = w_ref[...]

    def to_keys(logits):
        lg = jnp.maximum(logits, 0.0) * w
        s = lg[0:ts]
        for h in range(1, IDX_HEADS):
            s = s + lg[h * ts:(h + 1) * ts]
        return _sort_key(s)

    ikn = jnp.concatenate([ikn_ref[...], jnp.zeros((PAGE_SIZE - ts, IDX_DIM), F32)], axis=0)
    key = to_keys(_dot_nt(iq, ikn.astype(BF16)))
    t_idx = lax.broadcasted_iota(I32, (ts, PAGE_SIZE), 0)
    j_idx = lax.broadcasted_iota(I32, (ts, PAGE_SIZE), 1)
    key = jnp.where(j_idx <= t_idx, key, INT_MIN)
    key_scr[0] = jnp.concatenate([key, jnp.full((ts, width - PAGE_SIZE), INT_MIN, I32)], axis=1)
    for sl in range(groups):
        key_scr[sl + 1] = jnp.concatenate(
            [to_keys(_dot(iq, r[...].astype(BF16)))
             for r in page_refs[sl * per_slot:(sl + 1) * per_slot]], axis=1)

    def _select():
        def count(pred):
            acc = jnp.zeros((ts, width), F32)
            for sl in range(groups + 1):
                acc = acc + jnp.where(pred(key_scr[sl]), 1.0, 0.0)
            return jnp.sum(acc, axis=-1, keepdims=True)

        def search(it, cur):
            cand = cur + jnp.left_shift(jnp.int32(1), 31 - it)
            return jnp.where(count(lambda key: key >= cand) >= n_sel, cand, cur)

        cur = lax.fori_loop(0, 32, search, jnp.full((ts, 1), INT_MIN, I32))
        thr = jnp.maximum(cur, INT_MIN + 1)
        ties = jnp.max(count(lambda key: key >= thr)) > n_sel

        @pl.when(jnp.logical_not(ties))
        def _by_threshold():
            for sl in range(groups + 1):
                bias_ref[sl] = jnp.where(key_scr[sl] >= thr, 0.0, NEG)

        @pl.when(ties)
        def _by_rank():
            need = n_sel - count(lambda key: key > thr)
            lmat = l_ref[...]
            seen = jnp.zeros((ts, 1), F32)
            for sl in list(range(1, groups + 1)) + [0]:
                key = key_scr[sl]
                eq = key == thr
                eqf = jnp.where(eq, 1.0, 0.0)
                eq16 = jnp.concatenate([eqf, jnp.zeros_like(eqf)], axis=0).astype(BF16)
                rank = _dot(eq16, lmat)[0:ts] + seen
                sel = (key > thr) | (eq & (rank < need))
                bias_ref[sl] = jnp.where(sel, 0.0, NEG)
                seen = seen + jnp.sum(eqf, axis=-1, keepdims=True)

    _select()


def _dec_attn_kernel(pt_ref, *refs, mode, pages, groups, ts, lam_init):
    it = iter(refs)
    qbd_ref, kn_ref, vn_ref = next(it), next(it), next(it)
    bias_new_ref, bias_ref = (next(it), next(it)) if mode == "a" else (None, None)
    u_ref = next(it) if mode == "b" else None
    lam_ref, gn_ref = (next(it), next(it)) if mode == "c" else (None, None)
    kt_refs = [next(it) for _ in range(pages)]
    vt_refs = [next(it) for _ in range(pages)]
    o_ref = next(it)
    m_scr, l_scr, acc_scr = next(it), next(it), next(it)
    g = pl.program_id(1)
    rows = 2 * N_PAIRS * ts
    qbd = qbd_ref[...]

    def rows_of(x):
        return jnp.concatenate([x] * (2 * N_PAIRS), axis=0)

    def step(s, pvs, valid, u):
        if mode == "b":
            sp = _softplus(s)
            spm = sp if valid is None else jnp.where(valid, sp, 0.0)
            cs = _dot(spm.astype(BF16), u)
            w = jnp.exp(s - sp - (cs + m_scr[...]))
            if valid is not None:
                w = jnp.where(valid, w, 0.0)
            m_scr[...] += cs[:, 0:1] + spm[:, 0:1]
            p = w.astype(BF16)
            acc = acc_scr[...]
        else:
            if valid is not None:
                s = jnp.where(valid, s, NEG)
            m_old = m_scr[...]
            m_new = jnp.maximum(m_old, jnp.max(s, axis=-1, keepdims=True))
            a = jnp.exp(m_old - m_new)
            pf = jnp.exp(s - m_new)
            l_scr[...] = a * l_scr[...] + jnp.sum(pf, axis=-1, keepdims=True)
            m_scr[...] = m_new
            p = pf.astype(BF16)
            acc = a * acc_scr[...]
        for j, pv in enumerate(pvs):
            acc = acc + pv(p[:, j * PAGE_SIZE:(j + 1) * PAGE_SIZE])
        acc_scr[...] = acc

    @pl.when(g == 0)
    def _new_keys():
        m_scr[...] = jnp.full((rows, 1), 0.0 if mode == "b" else NEG, F32)
        l_scr[...] = jnp.zeros((rows, 1), F32)
        acc_scr[...] = jnp.zeros((rows, TOK_WIDTH), F32)
        pad = jnp.zeros((PAGE_SIZE - ts, TOK_WIDTH), F32)
        kn = jnp.concatenate([kn_ref[...], pad], axis=0).astype(BF16)
        vn = jnp.concatenate([vn_ref[...], pad], axis=0).astype(BF16)
        s = _dot_nt(qbd, kn)
        t_idx = rows_of(lax.broadcasted_iota(I32, (ts, PAGE_SIZE), 0))
        j_idx = lax.broadcasted_iota(I32, (rows, PAGE_SIZE), 1)
        valid = (j_idx < t_idx) if mode == "b" else (j_idx <= t_idx)
        if mode == "a":
            s = s + rows_of(bias_new_ref[:, 0:PAGE_SIZE])
        step(s, [lambda p: _dot(p, vn)], valid,
             u_ref[0:PAGE_SIZE, 0:PAGE_SIZE] if mode == "b" else None)

    s = jnp.concatenate([_dot(qbd, r[...].astype(BF16)) for r in kt_refs], axis=1)
    if mode == "a":
        s = s + rows_of(bias_ref[...])
    pvs = [functools.partial(lambda p, r: _dot_nt(p, r[...].astype(BF16)), r=r) for r in vt_refs]
    step(s, pvs, None, u_ref[...] if mode == "b" else None)

    @pl.when(g == groups - 1)
    def _finish():
        acc = acc_scr[...]
        if mode != "b":
            acc = acc / l_scr[...]
        lo_half = _lane_iota() < HEAD_DIM
        pieces = []
        for p in range(N_PAIRS):
            blk = acc[2 * p * ts:(2 * p + 2) * ts, p * LANES:(p + 1) * LANES]
            first, second = blk[0:ts], blk[ts:2 * ts]
            if mode == "c":
                o = first - _diff_lambda(lam_ref[...], lam_init) * second
                o = o * lax.rsqrt(jnp.mean(o * o, axis=-1, keepdims=True) + RMS_EPS)
                pieces.append(o * gn_ref[...] * (1.0 - lam_init))
            else:
                pieces.append(jnp.where(lo_half, first, second))
        o_ref[...] = jnp.concatenate(pieces, axis=1)


def _pages_per_step(n_pages):
    p = 8
    while n_pages % p:
        p //= 2
    return p


def _dec_indexer(page_table, cache_ikt, slot, iq_rows, w_rows, ik_new, n_sel, name):
    bs, n_pages = page_table.shape
    ts = ik_new.shape[1]
    groups = n_pages // _pages_per_step(n_pages)
    width = _pages_per_step(n_pages) * PAGE_SIZE
    lmat = _strict_lower(width).T

    def page_spec(p):
        return pl.BlockSpec((None, None, IDX_DIM, PAGE_SIZE),
                            lambda b, pt: (slot, pt[b, p], 0, 0))

    grid_spec = pltpu.PrefetchScalarGridSpec(
        num_scalar_prefetch=1,
        grid=(bs,),
        in_specs=[
            pl.BlockSpec((None, IDX_HEADS * ts, IDX_DIM), lambda b, pt: (b, 0, 0)),
            pl.BlockSpec((None, IDX_HEADS * ts, 1), lambda b, pt: (b, 0, 0)),
            pl.BlockSpec((None, ts, IDX_DIM), lambda b, pt: (b, 0, 0)),
            pl.BlockSpec((width, width), lambda b, pt: (0, 0)),
        ] + [page_spec(p) for p in range(n_pages)],
        out_specs=pl.BlockSpec((None, groups + 1, ts, width), lambda b, pt: (b, 0, 0, 0)),
        scratch_shapes=[pltpu.VMEM((groups + 1, ts, width), I32)],
    )
    kern = functools.partial(_dec_idx_kernel, pages=n_pages, groups=groups, n_sel=float(n_sel))
    return pl.pallas_call(
        kern,
        grid_spec=grid_spec,
        out_shape=jax.ShapeDtypeStruct((bs, groups + 1, ts, width), F32),
        compiler_params=_cparams(("parallel",), 32),
        name=name,
    )(page_table, iq_rows, w_rows, ik_new, lmat, *([cache_ikt] * n_pages))


def _dec_mixer(mode, layer, page_table, cache_kt, cache_vt, qbd, k_new, v_new, *, extra, name):
    bs, n_pages = page_table.shape
    ts = k_new.shape[1]
    pages = _pages_per_step(n_pages)
    groups = n_pages // pages
    width = pages * PAGE_SIZE
    rows = 2 * N_PAIRS * ts
    descending = mode == "b"

    def page_spec(p):
        def imap(b, g, pt):
            grp = groups - 1 - g if descending else g
            return (layer, pt[b, grp * pages + p], 0, 0)
        return pl.BlockSpec((None, None, TOK_WIDTH, PAGE_SIZE), imap)

    in_specs = [
        pl.BlockSpec((None, rows, TOK_WIDTH), lambda b, g, pt: (b, 0, 0)),
        pl.BlockSpec((None, ts, TOK_WIDTH), lambda b, g, pt: (b, 0, 0)),
        pl.BlockSpec((None, ts, TOK_WIDTH), lambda b, g, pt: (b, 0, 0)),
    ]
    args = [qbd, k_new, v_new]
    lam_init = 0.0
    if mode == "a":
        in_specs += [pl.BlockSpec((None, None, ts, width), lambda b, g, pt: (b, 0, 0, 0)),
                     pl.BlockSpec((None, None, ts, width), lambda b, g, pt: (b, g + 1, 0, 0))]
        args += [extra, extra]
    elif mode == "b":
        in_specs.append(pl.BlockSpec((width, width), lambda b, g, pt: (0, 0)))
        args.append(_strict_lower(width))
    else:
        lam_p, norm_g, lam_init = extra
        in_specs += [pl.BlockSpec(lam_p.shape, lambda b, g, pt: (0, 0)),
                     pl.BlockSpec((1, LANES), lambda b, g, pt: (0, 0))]
        args += [lam_p, norm_g.reshape(1, LANES)]
    in_specs += [page_spec(p) for p in range(pages)] * 2
    args += [cache_kt] * pages + [cache_vt] * pages
    grid_spec = pltpu.PrefetchScalarGridSpec(
        num_scalar_prefetch=1,
        grid=(bs, groups),
        in_specs=in_specs,
        out_specs=pl.BlockSpec((None, ts, TOK_WIDTH), lambda b, g, pt: (b, 0, 0)),
        scratch_shapes=[pltpu.VMEM((rows, 1), F32), pltpu.VMEM((rows, 1), F32),
                        pltpu.VMEM((rows, TOK_WIDTH), F32)],
    )
    kern = functools.partial(_dec_attn_kernel, mode=mode, pages=pages, groups=groups, ts=ts,
                             lam_init=lam_init)
    return pl.pallas_call(
        kern,
        grid_spec=grid_spec,
        out_shape=jax.ShapeDtypeStruct((bs, ts, TOK_WIDTH), F32),
        compiler_params=_cparams(("parallel", "arbitrary"), 48),
        name=name,
    )(page_table, *args)


def _oproj_ln_kernel(tok_ref, mem_ref, x_ref, wt_ref, wm_ref, g_ref, b_ref, o_ref, obf_ref, *, alpha):
    mix = (_dot(tok_ref[...].astype(BF16), wt_ref[...])
           + _dot(mem_ref[...].astype(BF16), wm_ref[...]))
    out = _layer_norm(alpha * x_ref[...] + mix, g_ref[...], b_ref[...])
    o_ref[...] = out
    obf_ref[...] = out.astype(BF16)


def _oproj_ln(tok, mem, x, w_tok, w_mem, g, b, alpha, name):
    m_rows, d = x.shape
    tm = _tiles(m_rows)["proj"]
    row = lambda n: pl.BlockSpec((tm, n), lambda i: (i, 0))
    full = lambda a: pl.BlockSpec(a.shape, lambda i: (0, 0))
    return pl.pallas_call(
        functools.partial(_oproj_ln_kernel, alpha=alpha),
        grid=(m_rows // tm,),
        in_specs=[row(TOK_WIDTH), row(MEM_WIDTH), row(d), full(w_tok), full(w_mem), full(g), full(b)],
        out_specs=[row(d), row(d)],
        out_shape=[jax.ShapeDtypeStruct((m_rows, d), F32), jax.ShapeDtypeStruct((m_rows, d), BF16)],
        compiler_params=_cparams(("parallel",), 40),
        name=name,
    )(tok, mem, x, w_tok, w_mem, g, b)


def _ffn_ln_kernel(xbf_ref, x_ref, wg_ref, wu_ref, wd_ref, g_ref, b_ref, o_ref, obf_ref,
                   *, alpha, halves):
    rows = xbf_ref.shape[0] // halves
    for r in range(halves):
        sl = slice(r * rows, (r + 1) * rows)
        xb = xbf_ref[sl, :]
        hg = _dot(xb, wg_ref[...])
        hu = _dot(xb, wu_ref[...])
        act = hg * (1.0 / (1.0 + jnp.exp(-hg))) * hu
        y = _dot(act.astype(BF16), wd_ref[...])
        out = _layer_norm(alpha * x_ref[sl, :] + y, g_ref[...], b_ref[...])
        o_ref[sl, :] = out
        obf_ref[sl, :] = out.astype(BF16)


def _ffn_ln(xbf, x, w_gu, w_d, g, b, alpha, name):
    m_rows, d = x.shape
    d_ff = w_d.shape[0]
    tm = _tiles(m_rows)["ffn"]
    halves = 2
    row = lambda: pl.BlockSpec((tm, d), lambda i: (i, 0))
    vec = lambda a: pl.BlockSpec(a.shape, lambda i: (0, 0))
    once = pl.Buffered(1)
    return pl.pallas_call(
        functools.partial(_ffn_ln_kernel, alpha=alpha, halves=halves),
        grid=(m_rows // tm,),
        in_specs=[row(), row(),
                  pl.BlockSpec((d, d_ff), lambda i: (0, 0), pipeline_mode=once),
                  pl.BlockSpec((d, d_ff), lambda i: (0, 1), pipeline_mode=once),
                  pl.BlockSpec((d_ff, d), lambda i: (0, 0), pipeline_mode=once),
                  vec(g), vec(b)],
        out_specs=[row(), row()],
        out_shape=[jax.ShapeDtypeStruct((m_rows, d), F32), jax.ShapeDtypeStruct((m_rows, d), BF16)],
        compiler_params=_cparams(("parallel",), 56),
        name=name,
    )(xbf, x, w_gu, w_gu, w_d, g, b)


def _rope_tables(pos):
    half = HEAD_DIM // 2
    inv = ROPE_THETA ** (-jnp.arange(half, dtype=F32) / half)
    ang = pos.astype(F32)[:, None] * inv[None, :]
    cos = jnp.cos(ang)
    sin = jnp.sin(ang)
    return jnp.tile(cos, (1, 4)), jnp.tile(jnp.concatenate([-sin, sin], axis=1), (1, 2))


def _heads_last(xt, n_heads):
    l, b, _, p = xt.shape
    return xt.reshape(l, b, n_heads, HEAD_DIM, p).transpose(0, 1, 4, 2, 3)


def kernel(x_prompt, x_sample, mem_prompt, cache_k, cache_v, cache_idx_k, cache_mem_k, cache_mem_v, page_table, w_in, w_idx, w_mem_kv, w_o, ln_mix_g, ln_mix_b, w_gate_up, w_down, ln_ffn_g, ln_ffn_b, diff_lambda, diff_norm_g):
    bp, tp, d_model = x_prompt.shape
    bs, ts, _ = x_sample.shape
    n_pages = page_table.shape[1]
    past = n_pages * PAGE_SIZE
    depth = w_in.shape[0]
    n_mem = mem_prompt.shape[1]
    n_pool = cache_k.shape[1]
    n_heads = 2 * N_PAIRS
    mem_heads = MEM_WIDTH // HEAD_DIM
    alpha = (2 * depth) ** 0.25
    nsel_p = min(TOPK_MAX, tp // 4)
    nsel_s = min(TOPK_MAX, (past + ts) // 4)
    mp, ms = bp * tp, bs * ts

    w_in_b = w_in.astype(BF16)
    w_mkv_b = w_mem_kv.astype(BF16)
    w_o_b = w_o.astype(BF16)
    w_gu_b = w_gate_up.astype(BF16)
    w_d_b = w_down.astype(BF16)
    iq_w, ik_w, ih_w = (w_idx[..., :IDX_Q_WIDTH], w_idx[..., IDX_Q_WIDTH:IDX_Q_WIDTH + IDX_DIM],
                        w_idx[..., IDX_Q_WIDTH + IDX_DIM:])
    w_idx_b = jnp.concatenate(
        [iq_w, ik_w, ik_w, ih_w, jnp.zeros(ih_w.shape[:2] + (LANES - IDX_HEADS,), F32)],
        axis=-1).astype(BF16)
    idx_cols = IDX_Q_WIDTH + 2 * LANES

    rope_p = _rope_tables(jnp.arange(tp))
    rope_s = _rope_tables(jnp.tile(past + jnp.arange(ts), bs))
    cache_kt = cache_k.transpose(0, 1, 3, 4, 2).reshape(depth, n_pool, TOK_WIDTH, PAGE_SIZE)
    cache_vt = cache_v.transpose(0, 1, 3, 4, 2).reshape(depth, n_pool, TOK_WIDTH, PAGE_SIZE)
    cache_ikt = cache_idx_k.transpose(0, 1, 3, 2)
    cache_mkt = cache_mem_k.transpose(0, 1, 3, 4, 2).reshape(depth, bs, MEM_WIDTH, n_mem)
    cache_mvt = cache_mem_v.transpose(0, 1, 3, 4, 2).reshape(depth, bs, MEM_WIDTH, n_mem)
    mem2d = mem_prompt.reshape(bp * n_mem, d_model)
    head_of_col = jnp.arange(TOK_WIDTH) // HEAD_DIM
    bd_mask = (jnp.arange(n_heads)[:, None, None] == head_of_col[None, None, :])

    xp = x_prompt.reshape(mp, d_model)
    xs = x_sample.reshape(ms, d_model)
    xp_in, xs_in = xp, xs
    nk_p, nv_p, nidx_p, nmk_p, nmv_p, nk_s, nv_s, nidx_s = [], [], [], [], [], [], [], []

    for l in range(depth):
        kind = l % N_MIXERS
        slot = l // N_MIXERS
        n_rope = n_heads if kind != 1 else 0
        tag = f"l{l}"
        c_q, c_k, c_v, c_m = 0, TOK_WIDTH, 2 * TOK_WIDTH, 3 * TOK_WIDTH
        in_w = c_m + MEM_WIDTH

        q_p, kt_f, kt_p, vt_f, v_p, mq_p = _proj(
            xp_in, w_in_b[l],
            (("pair", c_q, c_k, QK_SCALE, BF16), ("T", c_k, c_v, 1.0, F32),
             ("Ttile", c_k, c_v, 1.0, BF16), ("T", c_v, c_m, 1.0, F32),
             ("pair", c_v, c_m, 1.0, BF16), ("flat", c_m, in_w, QK_SCALE, BF16)),
            batch=bp, rope=rope_p, n_rope_chunks=n_rope, name=f"proj_p_{tag}")
        nk_p.append(kt_f)
        nv_p.append(vt_f)
        mkt_f, mvt_f = _proj(
            mem2d, w_mkv_b[l],
            (("T", 0, MEM_WIDTH, 1.0, F32), ("T", MEM_WIDTH, 2 * MEM_WIDTH, 1.0, F32)),
            batch=bp, name=f"proj_mem_{tag}")
        nmk_p.append(mkt_f)
        nmv_p.append(mvt_f)
        mem_p = _mem_attn(mq_p.reshape(bp, tp, MEM_WIDTH), mkt_f, mvt_f, None, BF16, f"mem_p_{tag}")

        q_s, k_s, v_s, mq_s = _proj(
            xs_in, w_in_b[l],
            (("flat", c_q, c_k, QK_SCALE, F32), ("flat", c_k, c_v, 1.0, F32),
             ("flat", c_v, c_m, 1.0, F32), ("flat", c_m, in_w, QK_SCALE, F32)),
            batch=bs, rope=rope_s, n_rope_chunks=n_rope, name=f"proj_s_{tag}")
        k_s3 = k_s.reshape(bs, ts, TOK_WIDTH)
        v_s3 = v_s.reshape(bs, ts, TOK_WIDTH)
        nk_s.append(k_s3.reshape(bs, ts, n_heads, HEAD_DIM))
        nv_s.append(v_s3.reshape(bs, ts, n_heads, HEAD_DIM))
        qbd = jnp.where(bd_mask[None], q_s.reshape(bs, 1, ts, TOK_WIDTH), 0.0)
        qbd = qbd.reshape(bs, n_heads * ts, TOK_WIDTH).astype(BF16)
        mem_s = _mem_attn(mq_s.reshape(bs, ts, MEM_WIDTH), cache_mkt, cache_mvt, l, F32,
                          f"mem_s_{tag}")

        if kind == 0:
            iq_p, ikt_p, ikt_f, iw_p = _proj(
                xp_in, w_idx_b[slot],
                (("flat", 0, IDX_Q_WIDTH, 1.0, BF16),
                 ("Ttile", IDX_Q_WIDTH, IDX_Q_WIDTH + LANES, 1.0, BF16),
                 ("T", IDX_Q_WIDTH, IDX_Q_WIDTH + IDX_DIM, 1.0, F32),
                 ("flat", IDX_Q_WIDTH + LANES, idx_cols, IDX_W_SCALE, F32)),
                batch=bp, rope=rope_p, n_rope_chunks=IDX_Q_WIDTH // LANES + 1, name=f"proj_ip_{tag}")
            nidx_p.append(ikt_f)
            tok_p = _prompt_mixer(
                0, q_p, kt_p, v_p,
                extra=(iq_p.reshape(bp, tp, IDX_Q_WIDTH), ikt_p, iw_p.reshape(bp, tp, LANES), nsel_p),
                name=f"mix_a_{tag}")

            iq_s, ik_s, iw_s = _proj(
                xs_in, w_idx_b[slot],
                (("flat", 0, IDX_Q_WIDTH, 1.0, F32),
                 ("flat", IDX_Q_WIDTH, IDX_Q_WIDTH + IDX_DIM, 1.0, F32),
                 ("flat", IDX_Q_WIDTH + LANES, idx_cols, IDX_W_SCALE, F32)),
                batch=bs, rope=rope_s, n_rope_chunks=IDX_Q_WIDTH // LANES + 1, name=f"proj_is_{tag}")
            ik_s3 = ik_s.reshape(bs, ts, IDX_DIM)
            nidx_s.append(ik_s3)
            iq_rows = iq_s.reshape(bs, ts, IDX_HEADS, IDX_DIM).swapaxes(1, 2)
            iq_rows = iq_rows.reshape(bs, IDX_HEADS * ts, IDX_DIM).astype(BF16)
            w_rows = iw_s.reshape(bs, ts, LANES)[:, :, :IDX_HEADS].swapaxes(1, 2)
            w_rows = w_rows.reshape(bs, IDX_HEADS * ts, 1)
            bias = _dec_indexer(page_table, cache_ikt, slot, iq_rows, w_rows, ik_s3, nsel_s,
                                f"idx_s_{tag}")
            tok_s = _dec_mixer("a", l, page_table, cache_kt, cache_vt, qbd, k_s3, v_s3,
                               extra=bias, name=f"mix_sa_{tag}")
        elif kind == 1:
            tok_p = _prompt_mixer(1, q_p, kt_p, v_p, extra=None, name=f"mix_b_{tag}")
            tok_s = _dec_mixer("b", l, page_table, cache_kt, cache_vt, qbd, k_s3, v_s3,
                               extra=None, name=f"mix_sb_{tag}")
        else:
            lam_init = 0.8 - 0.6 * math.exp(-0.3 * l)
            extra = (diff_lambda[slot], diff_norm_g[slot], lam_init)
            tok_p = _prompt_mixer(2, q_p, kt_p, v_p, extra=extra, name=f"mix_c_{tag}")
            tok_s = _dec_mixer("c", l, page_table, cache_kt, cache_vt, qbd, k_s3, v_s3,
                               extra=extra, name=f"mix_sc_{tag}")

        w_tok, w_mem = w_o_b[l, :TOK_WIDTH], w_o_b[l, TOK_WIDTH:]
        g1, b1 = ln_mix_g[l].reshape(1, d_model), ln_mix_b[l].reshape(1, d_model)
        g2, b2 = ln_ffn_g[l].reshape(1, d_model), ln_ffn_b[l].reshape(1, d_model)
        x1, x1b = _oproj_ln(tok_p.reshape(mp, TOK_WIDTH), mem_p.reshape(mp, MEM_WIDTH), xp,
                            w_tok, w_mem, g1, b1, alpha, f"oproj_p_{tag}")
        xp, xp_in = _ffn_ln(x1b, x1, w_gu_b[l], w_d_b[l], g2, b2, alpha, f"ffn_p_{tag}")
        y1, y1b = _oproj_ln(tok_s.reshape(ms, TOK_WIDTH), mem_s.reshape(ms, MEM_WIDTH), xs,
                            w_tok, w_mem, g1, b1, alpha, f"oproj_s_{tag}")
        xs, xs_in = _ffn_ln(y1b, y1, w_gu_b[l], w_d_b[l], g2, b2, alpha, f"ffn_s_{tag}")

    return (xp.reshape(bp, tp, d_model), xs.reshape(bs, ts, d_model),
            _heads_last(jnp.stack(nk_p), n_heads), _heads_last(jnp.stack(nv_p), n_heads),
            jnp.stack(nidx_p).transpose(0, 1, 3, 2),
            _heads_last(jnp.stack(nmk_p), mem_heads), _heads_last(jnp.stack(nmv_p), mem_heads),
            jnp.stack(nk_s), jnp.stack(nv_s), jnp.stack(nidx_s))
```

```python
import functools
import math

import jax
import jax.numpy as jnp
from jax import lax
from jax.experimental import pallas as pl
from jax.experimental.pallas import tpu as pltpu

F32 = jnp.float32
BF16 = jnp.bfloat16
I32 = jnp.int32

HEAD_DIM = 64
TOK_WIDTH = 768
N_PAIRS = 6
MEM_WIDTH = 256
IDX_HEADS = 8
IDX_DIM = 64
IDX_Q_WIDTH = IDX_HEADS * IDX_DIM
IDX_W_SCALE = IDX_HEADS ** -0.5 * IDX_DIM ** -0.5
PAGE_SIZE = 128
TOPK_MAX = 256
N_MIXERS = 3
ROPE_THETA = 10000.0
LN_EPS = 1e-5
RMS_EPS = 1e-5
QK_SCALE = HEAD_DIM ** -0.5

LANES = 128
NEG = -1e30
STICK_EXIT = 110.0
INT_MIN = -2 ** 31
HALF16 = 2 ** 15
I16 = jnp.int16
MIB = 1024 * 1024


def _tiles(rows):
    attn = min(512, rows)
    return dict(proj=min(512, rows), ffn=min(512, rows), attn=attn, chunk=min(2, rows // attn))


def _cparams(sem, vmem_mib):
    return pltpu.CompilerParams(dimension_semantics=sem, vmem_limit_bytes=vmem_mib * MIB)


def _dot(a, b):
    return jnp.dot(a, b, preferred_element_type=F32)


def _dot_nt(a, b):
    return lax.dot_general(a, b, (((1,), (1,)), ((), ())), preferred_element_type=F32)


def _lane_iota():
    return lax.broadcasted_iota(I32, (1, LANES), 1)


def _layer_norm(y, g, b):
    mu = jnp.mean(y, axis=-1, keepdims=True)
    d = y - mu
    var = jnp.mean(d * d, axis=-1, keepdims=True)
    return d * lax.rsqrt(var + LN_EPS) * g + b


def _softmax_update(s, m, l, acc, pv):
    m_new = jnp.maximum(m, jnp.max(s, axis=-1, keepdims=True))
    a = jnp.exp(m - m_new)
    p = jnp.exp(s - m_new)
    l = a * l + jnp.sum(p, axis=-1, keepdims=True)
    acc = a * acc + pv(p.astype(BF16))
    return m_new, l, acc


def _sort_key(x):
    bits = lax.bitcast_convert_type(x, I32)
    return bits ^ ((bits >> 31) & 0x7FFFFFFF)


def _softplus(z):
    return jnp.maximum(z, 0.0) + jnp.log(1.0 + jnp.exp(-jnp.abs(z)))


def _diff_lambda(lp, lam_init):
    return (jnp.exp(jnp.sum(lp[0:1] * lp[1:2], axis=-1, keepdims=True))
            - jnp.exp(jnp.sum(lp[2:3] * lp[3:4], axis=-1, keepdims=True)) + lam_init)


def _head_split(q):
    lo_half = _lane_iota() < HEAD_DIM
    zero = jnp.zeros_like(q)
    return jnp.where(lo_half, q, zero), jnp.where(lo_half, zero, q)


def _proj_kernel(*refs, n_rope_chunks, outs, tk):
    x_ref, w_ref = refs[:2]
    if n_rope_chunks:
        cos_ref, sin_ref = refs[2:4]
    o_refs = refs[len(refs) - len(outs):]
    y = _dot(x_ref[...].astype(BF16), w_ref[...])
    n_chunks = y.shape[1] // LANES
    chunks = [y[:, c * LANES:(c + 1) * LANES] for c in range(n_chunks)]
    if n_rope_chunks:
        cos = cos_ref[...]
        sin = sin_ref[...]
        first = (_lane_iota() & (HEAD_DIM - 1)) < HEAD_DIM // 2
        for c in range(n_rope_chunks):
            yc = chunks[c]
            partner = jnp.where(first, pltpu.roll(yc, LANES - HEAD_DIM // 2, 1),
                                pltpu.roll(yc, HEAD_DIM // 2, 1))
            chunks[c] = yc * cos + partner * sin
    for o_ref, (kind, lo, hi, scale) in zip(o_refs, outs):
        c0, c1 = lo // LANES, -(-hi // LANES)
        if kind == "pair":
            for p in range(c1 - c0):
                v = chunks[c0 + p]
                o_ref[p] = (v * scale if scale != 1.0 else v).astype(o_ref.dtype)
            continue
        v = chunks[c0] if c1 - c0 == 1 else jnp.concatenate(chunks[c0:c1], axis=1)
        if scale != 1.0:
            v = v * scale
        if kind == "flat":
            o_ref[...] = v[:, lo - c0 * LANES:hi - c0 * LANES].astype(o_ref.dtype)
        elif kind == "T":
            o_ref[...] = v.T[0:hi - lo].astype(o_ref.dtype)
        else:
            vt = v.T.astype(o_ref.dtype)
            for j in range(vt.shape[1] // tk):
                o_ref[j] = vt[:, j * tk:(j + 1) * tk]


def _proj(x, w, outs, *, batch, rope=None, n_rope_chunks=0, name):
    m_rows, k_dim = x.shape
    n_cols = w.shape[1]
    t_len = m_rows // batch
    structured = any(o[0] != "flat" for o in outs)
    tm = _tiles(t_len if structured else m_rows)["proj"]
    tk = _tiles(t_len)["attn"]
    tiles_per_seq = max(t_len // tm, 1)
    in_specs = [pl.BlockSpec((tm, k_dim), lambda i: (i, 0)),
                pl.BlockSpec((k_dim, n_cols), lambda i: (0, 0))]
    args = [x, w]
    if n_rope_chunks:
        cos, sin = rope
        n_tab = cos.shape[0] // tm
        in_specs += [pl.BlockSpec((tm, LANES), lambda i: (i % n_tab, 0))] * 2
        args += [cos, sin]
    seq_of = lambda i: i // tiles_per_seq
    tile_of = lambda i: i % tiles_per_seq
    out_shapes, out_specs = [], []
    for kind, lo, hi, _, dtype in outs:
        if kind == "pair":
            groups = (hi - lo) // LANES
            shape, block = (batch, groups, t_len, LANES), (None, groups, tm, LANES)
            imap = lambda i: (seq_of(i), 0, tile_of(i), 0)
        elif kind == "T":
            shape, block = (batch, hi - lo, t_len), (None, hi - lo, tm)
            imap = lambda i: (seq_of(i), 0, tile_of(i))
        elif kind == "Ttile":
            shape, block = (batch, t_len // tk, hi - lo, tk), (None, tm // tk, hi - lo, tk)
            imap = lambda i: (seq_of(i), tile_of(i), 0, 0)
        else:
            shape, block = (m_rows, hi - lo), (tm, hi - lo)
            imap = lambda i: (i, 0)
        out_shapes.append(jax.ShapeDtypeStruct(shape, dtype))
        out_specs.append(pl.BlockSpec(block, imap))
    kern = functools.partial(_proj_kernel, n_rope_chunks=n_rope_chunks,
                             outs=tuple(o[:4] for o in outs), tk=tk)
    return pl.pallas_call(
        kern,
        grid=(m_rows // tm,),
        in_specs=in_specs,
        out_specs=out_specs,
        out_shape=out_shapes,
        compiler_params=_cparams(("parallel",), 56),
        name=name,
    )(*args)


def _mem_attn_kernel(q_ref, kt_ref, vt_ref, o_ref):
    lo_half = _lane_iota() < HEAD_DIM
    pieces = []
    for pr in range(MEM_WIDTH // LANES):
        sl = slice(pr * LANES, (pr + 1) * LANES)
        kt = kt_ref[sl, :].astype(BF16)
        vt = vt_ref[sl, :].astype(BF16)
        res = []
        for qc in _head_split(q_ref[:, sl].astype(BF16)):
            s = _dot(qc, kt)
            m = jnp.max(s, axis=-1, keepdims=True)
            p = jnp.exp(s - m)
            l = jnp.sum(p, axis=-1, keepdims=True)
            res.append(_dot_nt(p.astype(BF16), vt) / l)
        pieces.append(jnp.where(lo_half, res[0], res[1]))
    o_ref[...] = jnp.concatenate(pieces, axis=1).astype(o_ref.dtype)


def _mem_attn(q, kt, vt, layer, out_dtype, name):
    b, t, _ = q.shape
    tq = _tiles(t)["attn"]
    n_mem = kt.shape[-1]
    if layer is None:
        kv_spec = pl.BlockSpec((None, MEM_WIDTH, n_mem), lambda bi, i: (bi, 0, 0))
    else:
        kv_spec = pl.BlockSpec((None, None, MEM_WIDTH, n_mem), lambda bi, i: (layer, bi, 0, 0))
    return pl.pallas_call(
        _mem_attn_kernel,
        grid=(b, t // tq),
        in_specs=[pl.BlockSpec((None, tq, MEM_WIDTH), lambda bi, i: (bi, i, 0)), kv_spec, kv_spec],
        out_specs=pl.BlockSpec((None, tq, MEM_WIDTH), lambda bi, i: (bi, i, 0)),
        out_shape=jax.ShapeDtypeStruct((b, t, MEM_WIDTH), out_dtype),
        compiler_params=_cparams(("parallel", "parallel"), 32),
        name=name,
    )(q, kt, vt)


def _tile_iotas(t):
    return (lax.broadcasted_iota(I32, (t, t), 0), lax.broadcasted_iota(I32, (t, t), 1))


def _attn_c_kernel(lam_ref, g_ref, q_ref, kt_ref, v_ref, o_ref, *, tq, ct, lam_init):
    i = pl.program_id(2)
    diag_chunk = i // ct
    qs = _head_split(q_ref[...])

    row, col = _tile_iotas(tq)
    one_bf = jnp.ones((tq, LANES), BF16)

    def chunk(c, carry, diag):
        carry = list(carry)
        for j in range(ct):
            kt = c * ct + j
            k = kt_ref[kt]
            v = v_ref[pl.ds(pl.multiple_of(kt * tq, tq), tq), :]
            v1 = jnp.concatenate([v, one_bf], axis=1)
            for h in range(2):
                m, acc = carry[h]
                s = _dot(qs[h], k)
                if diag:
                    s = jnp.where(col + kt * tq <= row + i * tq, s, NEG)
                m_new = jnp.maximum(m, jnp.max(s, axis=-1, keepdims=True))
                p = jnp.exp(s - m_new).astype(BF16)
                carry[h] = (m_new, jnp.exp(m - m_new) * acc + _dot(p, v1))
        return tuple(carry)

    one = (jnp.full((tq, 1), NEG, F32), jnp.zeros((tq, 2 * LANES), F32))
    carry = lax.fori_loop(0, diag_chunk, lambda c, cr: chunk(c, cr, False), (one, one))
    (_, acc0), (_, acc1) = chunk(diag_chunk, carry, True)
    o = (acc0[:, :LANES] / acc0[:, LANES:]
         - _diff_lambda(lam_ref[...], lam_init) * (acc1[:, :LANES] / acc1[:, LANES:]))
    o = o * lax.rsqrt(jnp.mean(o * o, axis=-1, keepdims=True) + RMS_EPS)
    o_ref[...] = (o * g_ref[...] * (1.0 - lam_init)).astype(o_ref.dtype)


def _attn_b_kernel(u_ref, q_ref, kt_ref, v_ref, o_ref, *, tq):
    i = pl.program_id(2)
    qs = _head_split(q_ref[...])
    u = u_ref[...]
    row, col = _tile_iotas(tq)
    strict = col < row

    def tile(kt, carry, diag):
        k = kt_ref[kt]
        v = v_ref[pl.ds(pl.multiple_of(kt * tq, tq), tq), :]
        new = []
        for c in range(2):
            later, acc = carry[c]
            z = _dot(qs[c], k)
            sp = _softplus(z)
            spm = jnp.where(strict, sp, 0.0) if diag else sp
            cs = _dot(spm.astype(BF16), u)
            w = jnp.exp(z - sp - (cs + later))
            if diag:
                w = jnp.where(strict, w, 0.0)
            acc = acc + _dot(w.astype(BF16), v)
            later = later + cs[:, 0:1] + spm[:, 0:1]
            new.append((later, acc))
        return tuple(new)

    def live(carry):
        return (jnp.minimum(jnp.min(carry[0][0]), jnp.min(carry[1][0])) < STICK_EXIT).astype(I32)

    def cond(state):
        return (state[0] >= 0) & (state[1] > 0)

    def body(state):
        carry = tile(state[0], state[2], False)
        return state[0] - 1, live(carry), carry

    one = (jnp.zeros((tq, 1), F32), jnp.zeros((tq, LANES), F32))
    carry = tile(i, (one, one), True)
    carry = lax.while_loop(cond, body, (i - 1, live(carry), carry))[2]
    lo_half = _lane_iota() < HEAD_DIM
    o_ref[...] = jnp.where(lo_half, carry[0][1], carry[1][1]).astype(o_ref.dtype)


def _fold_lanes(c):
    out = c[:, 0:LANES]
    for j in range(1, c.shape[1] // LANES):
        out = out + c[:, j * LANES:(j + 1) * LANES]
    return out


def _attn_a_kernel(l_ref, iq_ref, ikt_ref, iw_ref, q_ref, kt_ref, v_ref, o_ref, key_scr, bias_scr,
                   hi_scr, lo_scr, *, tq, ct, n_sel):
    i = pl.program_id(1)
    pair = pl.program_id(2)
    n_chunks = i // ct + 1
    row, col = _tile_iotas(tq)
    causal = col <= row

    @pl.when(pair == 0)
    def _select():
        w_all = iw_ref[...]
        iq_heads = []
        for pr in range(IDX_Q_WIDTH // LANES):
            iq_heads += list(_head_split(iq_ref[:, pr * LANES:(pr + 1) * LANES]))

        def score_tile(kt, diag):
            ik = ikt_ref[kt]
            sc = jnp.zeros((tq, tq), F32)
            for h in range(IDX_HEADS):
                sc = sc + w_all[:, h:h + 1] * jnp.maximum(_dot(iq_heads[h], ik), 0.0)
            key = _sort_key(sc)
            if diag:
                key = jnp.where(causal, key, INT_MIN)
            key_scr[kt] = key
            hi_scr[kt] = (key >> 16).astype(I16)
            lo_scr[kt] = ((key & 0xFFFF) - HALF16).astype(I16)

        def _score_body(kt, c):
            score_tile(kt, False)
            return c

        lax.fori_loop(0, i, _score_body, 0)
        score_tile(i, True)

        def _pad_body(kt, c):
            bias_scr[kt] = jnp.full((tq, tq), NEG, F32)
            return c

        lax.fori_loop(i + 1, n_chunks * ct, _pad_body, 0)

        def count(pred):
            def body(kt, acc):
                return acc + _fold_lanes(jnp.where(pred(key_scr[kt]), 1.0, 0.0))
            acc = lax.fori_loop(0, i + 1, body, jnp.zeros((tq, LANES), F32))
            return jnp.sum(acc, axis=-1, keepdims=True)

        one16, zero16 = jnp.int16(1), jnp.int16(0)

        def count16(scr, bound, strict):
            b16 = jnp.broadcast_to(bound, (tq, tq)).astype(I16)

            def body(kt, acc):
                hit = (scr[kt] > b16) if strict else (scr[kt] >= b16)
                return acc + _fold_lanes(jnp.where(hit, one16, zero16))

            acc = lax.fori_loop(0, i + 1, body, jnp.zeros((tq, LANES), I16))
            return jnp.sum(acc.astype(F32), axis=-1, keepdims=True)

        def kth_largest16(scr, k):
            def step(it, cur):
                cand = cur + jnp.left_shift(jnp.int32(1), 15 - it)
                return jnp.where(count16(scr, cand, False) >= k, cand, cur)
            return lax.fori_loop(0, 16, step, jnp.full((tq, 1), -HALF16, I32))

        hi_thr = kth_largest16(hi_scr, n_sel)
        rank_lo = n_sel - count16(hi_scr, hi_thr, True)
        hi_thr16 = jnp.broadcast_to(hi_thr, (tq, tq)).astype(I16)

        def _keep_body(kt, c):
            lo_scr[kt] = jnp.where(hi_scr[kt] == hi_thr16, lo_scr[kt], jnp.int16(-HALF16))
            return c

        lax.fori_loop(0, i + 1, _keep_body, 0)
        lo_thr = kth_largest16(lo_scr, rank_lo)
        cur = jnp.left_shift(hi_thr, 16) + (lo_thr + HALF16)
        thr = jnp.maximum(cur, INT_MIN + 1)
        ties = jnp.max(count(lambda key: key >= thr)) > n_sel

        @pl.when(jnp.logical_not(ties))
        def _by_threshold():
            def body(kt, c):
                bias_scr[kt] = jnp.where(key_scr[kt] >= thr, 0.0, NEG)
                return c
            lax.fori_loop(0, i + 1, body, 0)

        @pl.when(ties)
        def _by_rank():
            need = n_sel - count(lambda key: key > thr)
            lmat = l_ref[...]

            def body(kt, seen):
                key = key_scr[kt]
                eq = key == thr
                eqf = jnp.where(eq, 1.0, 0.0)
                rank = _dot(eqf.astype(BF16), lmat) + seen
                sel = (key > thr) | (eq & (rank < need))
                bias_scr[kt] = jnp.where(sel, 0.0, NEG)
                return seen + jnp.sum(eqf, axis=-1, keepdims=True)

            lax.fori_loop(0, i + 1, body, jnp.zeros((tq, 1), F32))

    qs = _head_split(q_ref[...])
    lo_half = _lane_iota() < HEAD_DIM
    one_bf = jnp.ones((tq, LANES), BF16)

    def chunk(c, carry):
        carry = list(carry)
        for j in range(ct):
            kt = c * ct + j
            k = kt_ref[kt]
            v = v_ref[pl.ds(pl.multiple_of(kt * tq, tq), tq), :]
            vs = (jnp.where(lo_half, v, one_bf), jnp.where(lo_half, one_bf, v))
            bias = bias_scr[kt]
            for h in range(2):
                m, acc = carry[h]
                s = _dot(qs[h], k) + bias
                m_new = jnp.maximum(m, jnp.max(s, axis=-1, keepdims=True))
                p = jnp.exp(s - m_new).astype(BF16)
                carry[h] = (m_new, jnp.exp(m - m_new) * acc + _dot(p, vs[h]))
        return tuple(carry)

    one = (jnp.full((tq, 1), NEG, F32), jnp.zeros((tq, LANES), F32))
    (_, a0), (_, a1) = lax.fori_loop(0, n_chunks, chunk, (one, one))
    half = HEAD_DIM
    o_ref[...] = jnp.where(lo_half, a0 / pltpu.roll(a0, half, 1),
                           a1 / pltpu.roll(a1, half, 1)).astype(o_ref.dtype)


def _strict_lower(n):
    r = lax.broadcasted_iota(I32, (n, n), 0)
    c = lax.broadcasted_iota(I32, (n, n), 1)
    return (r > c).astype(BF16)


def _prompt_mixer(kind, q, kt, v, *, extra, name):
    b, _, t, _ = q.shape
    tq = _tiles(t)["attn"]
    ct = _tiles(t)["chunk"]
    nq = t // tq
    assert nq % ct == 0
    out_shape = jax.ShapeDtypeStruct((b, t, TOK_WIDTH), BF16)
    if kind == 0:
        iq, ikt, iw, n_sel = extra
        lmat = _strict_lower(tq).T
        kern = functools.partial(_attn_a_kernel, tq=tq, ct=ct, n_sel=float(n_sel))
        return pl.pallas_call(
            kern,
            grid=(b, nq, N_PAIRS),
            in_specs=[
                pl.BlockSpec((tq, tq), lambda bi, i, p: (0, 0)),
                pl.BlockSpec((None, tq, IDX_Q_WIDTH), lambda bi, i, p: (bi, i, 0)),
                pl.BlockSpec((None, nq, LANES, tq), lambda bi, i, p: (bi, 0, 0, 0)),
                pl.BlockSpec((None, tq, LANES), lambda bi, i, p: (bi, i, 0)),
                pl.BlockSpec((None, None, tq, LANES), lambda bi, i, p: (bi, p, i, 0)),
                pl.BlockSpec((None, nq, LANES, tq), lambda bi, i, p: (bi, 0, p, 0)),
                pl.BlockSpec((None, None, t, LANES), lambda bi, i, p: (bi, p, 0, 0)),
            ],
            out_specs=pl.BlockSpec((None, tq, LANES), lambda bi, i, p: (bi, i, p)),
            out_shape=out_shape,
            scratch_shapes=[pltpu.VMEM((nq, tq, tq), I32), pltpu.VMEM((nq, tq, tq), F32),
                            pltpu.VMEM((nq, tq, tq), I16), pltpu.VMEM((nq, tq, tq), I16)],
            compiler_params=_cparams(("parallel", "parallel", "arbitrary"), 48),
            name=name,
        )(lmat, iq, ikt, iw, q, kt, v)
    qkv_specs = [
        pl.BlockSpec((None, None, tq, LANES), lambda bi, p, i: (bi, p, i, 0)),
        pl.BlockSpec((None, nq, LANES, tq), lambda bi, p, i: (bi, 0, p, 0)),
        pl.BlockSpec((None, None, t, LANES), lambda bi, p, i: (bi, p, 0, 0)),
    ]
    out_spec = pl.BlockSpec((None, tq, LANES), lambda bi, p, i: (bi, i, p))
    if kind == 1:
        kern = functools.partial(_attn_b_kernel, tq=tq)
        consts = [_strict_lower(tq)]
        const_specs = [pl.BlockSpec((tq, tq), lambda bi, p, i: (0, 0))]
    else:
        lam_p, norm_g, lam_init = extra
        kern = functools.partial(_attn_c_kernel, tq=tq, ct=ct, lam_init=lam_init)
        consts = [lam_p, norm_g.reshape(1, LANES)]
        const_specs = [pl.BlockSpec(lam_p.shape, lambda bi, p, i: (0, 0)),
                       pl.BlockSpec((1, LANES), lambda bi, p, i: (0, 0))]
    return pl.pallas_call(
        kern,
        grid=(b, N_PAIRS, nq),
        in_specs=const_specs + qkv_specs,
        out_specs=out_spec,
        out_shape=out_shape,
        compiler_params=_cparams(("parallel", "parallel", "parallel"), 32),
        name=name,
    )(*consts, q, kt, v)


def _dec_idx_kernel(pt_ref, iq_ref, w_ref, ikn_ref, l_ref, *rest, pages, groups, n_sel):
    page_refs = rest[:pages]
    bias_ref = rest[pages]
    key_scr = rest[pages + 1]
    ts = bias_ref.shape[1]
    per_slot = pages // groups
    width = per_slot * PAGE_SIZE
    iq = iq_ref[...]
    w = w_ref[...]

    def to_keys(logits):
        lg = jnp.maximum(logits, 0.0) * w
        s = lg[0:ts]
        for h in range(1, IDX_HEADS):
            s = s + lg[h * ts:(h + 1) * ts]
        return _sort_key(s)

    ikn = jnp.concatenate([ikn_ref[...], jnp.zeros((PAGE_SIZE - ts, IDX_DIM), F32)], axis=0)
    key = to_keys(_dot_nt(iq, ikn.astype(BF16)))
    t_idx = lax.broadcasted_iota(I32, (ts, PAGE_SIZE), 0)
    j_idx = lax.broadcasted_iota(I32, (ts, PAGE_SIZE), 1)
    key = jnp.where(j_idx <= t_idx, key, INT_MIN)
    key_scr[0] = jnp.concatenate([key, jnp.full((ts, width - PAGE_SIZE), INT_MIN, I32)], axis=1)
    for sl in range(groups):
        key_scr[sl + 1] = jnp.concatenate(
            [to_keys(_dot(iq, r[...].astype(BF16)))
             for r in page_refs[sl * per_slot:(sl + 1) * per_slot]], axis=1)

    def _select():
        def count(pred):
            acc = jnp.zeros((ts, width), F32)
            for sl in range(groups + 1):
                acc = acc + jnp.where(pred(key_scr[sl]), 1.0, 0.0)
            return jnp.sum(acc, axis=-1, keepdims=True)

        def search(it, cur):
            cand = cur + jnp.left_shift(jnp.int32(1), 31 - it)
            return jnp.where(count(lambda key: key >= cand) >= n_sel, cand, cur)

        cur = lax.fori_loop(0, 32, search, jnp.full((ts, 1), INT_MIN, I32))
        thr = jnp.maximum(cur, INT_MIN + 1)
        ties = jnp.max(count(lambda key: key >= thr)) > n_sel

        @pl.when(jnp.logical_not(ties))
        def _by_threshold():
            for sl in range(groups + 1):
                bias_ref[sl] = jnp.where(key_scr[sl] >= thr, 0.0, NEG)

        @pl.when(ties)
        def _by_rank():
            need = n_sel - count(lambda key: key > thr)
            lmat = l_ref[...]
            seen = jnp.zeros((ts, 1), F32)
            for sl in list(range(1, groups + 1)) + [0]:
                key = key_scr[sl]
                eq = key == thr
                eqf = jnp.where(eq, 1.0, 0.0)
                eq16 = jnp.concatenate([eqf, jnp.zeros_like(eqf)], axis=0).astype(BF16)
                rank = _dot(eq16, lmat)[0:ts] + seen
                sel = (key > thr) | (eq & (rank < need))
                bias_ref[sl] = jnp.where(sel, 0.0, NEG)
                seen = seen + jnp.sum(eqf, axis=-1, keepdims=True)

    _select()


def _dec_attn_kernel(pt_ref, *refs, mode, pages, groups, ts, lam_init):
    it = iter(refs)
    qbd_ref, kn_ref, vn_ref = next(it), next(it), next(it)
    bias_new_ref, bias_ref = (next(it), next(it)) if mode == "a" else (None, None)
    u_ref = next(it) if mode == "b" else None
    lam_ref, gn_ref = (next(it), next(it)) if mode == "c" else (None, None)
    kt_refs = [next(it) for _ in range(pages)]
    vt_refs = [next(it) for _ in range(pages)]
    o_ref = next(it)
    m_scr, l_scr, acc_scr = next(it), next(it), next(it)
    g = pl.program_id(1)
    rows = 2 * N_PAIRS * ts
    qbd = qbd_ref[...]

    def rows_of(x):
        return jnp.concatenate([x] * (2 * N_PAIRS), axis=0)

    def step(s, pvs, valid, u):
        if mode == "b":
            sp = _softplus(s)
            spm = sp if valid is None else jnp.where(valid, sp, 0.0)
            cs = _dot(spm.astype(BF16), u)
            w = jnp.exp(s - sp - (cs + m_scr[...]))
            if valid is not None:
                w = jnp.where(valid, w, 0.0)
            m_scr[...] += cs[:, 0:1] + spm[:, 0:1]
            p = w.astype(BF16)
            acc = acc_scr[...]
        else:
            if valid is not None:
                s = jnp.where(valid, s, NEG)
            m_old = m_scr[...]
            m_new = jnp.maximum(m_old, jnp.max(s, axis=-1, keepdims=True))
            a = jnp.exp(m_old - m_new)
            pf = jnp.exp(s - m_new)
            l_scr[...] = a * l_scr[...] + jnp.sum(pf, axis=-1, keepdims=True)
            m_scr[...] = m_new
            p = pf.astype(BF16)
            acc = a * acc_scr[...]
        for j, pv in enumerate(pvs):
            acc = acc + pv(p[:, j * PAGE_SIZE:(j + 1) * PAGE_SIZE])
        acc_scr[...] = acc

    @pl.when(g == 0)
    def _new_keys():
        m_scr[...] = jnp.full((rows, 1), 0.0 if mode == "b" else NEG, F32)
        l_scr[...] = jnp.zeros((rows, 1), F32)
        acc_scr[...] = jnp.zeros((rows, TOK_WIDTH), F32)
        pad = jnp.zeros((PAGE_SIZE - ts, TOK_WIDTH), F32)
        kn = jnp.concatenate([kn_ref[...], pad], axis=0).astype(BF16)
        vn = jnp.concatenate([vn_ref[...], pad], axis=0).astype(BF16)
        s = _dot_nt(qbd, kn)
        t_idx = rows_of(lax.broadcasted_iota(I32, (ts, PAGE_SIZE), 0))
        j_idx = lax.broadcasted_iota(I32, (rows, PAGE_SIZE), 1)
        valid = (j_idx < t_idx) if mode == "b" else (j_idx <= t_idx)
        if mode == "a":
            s = s + rows_of(bias_new_ref[:, 0:PAGE_SIZE])
        step(s, [lambda p: _dot(p, vn)], valid,
             u_ref[0:PAGE_SIZE, 0:PAGE_SIZE] if mode == "b" else None)

    s = jnp.concatenate([_dot(qbd, r[...].astype(BF16)) for r in kt_refs], axis=1)
    if mode == "a":
        s = s + rows_of(bias_ref[...])
    pvs = [functools.partial(lambda p, r: _dot_nt(p, r[...].astype(BF16)), r=r) for r in vt_refs]
    step(s, pvs, None, u_ref[...] if mode == "b" else None)

    @pl.when(g == groups - 1)
    def _finish():
        acc = acc_scr[...]
        if mode != "b":
            acc = acc / l_scr[...]
        lo_half = _lane_iota() < HEAD_DIM
        pieces = []
        for p in range(N_PAIRS):
            blk = acc[2 * p * ts:(2 * p + 2) * ts, p * LANES:(p + 1) * LANES]
            first, second = blk[0:ts], blk[ts:2 * ts]
            if mode == "c":
                o = first - _diff_lambda(lam_ref[...], lam_init) * second
                o = o * lax.rsqrt(jnp.mean(o * o, axis=-1, keepdims=True) + RMS_EPS)
                pieces.append(o * gn_ref[...] * (1.0 - lam_init))
            else:
                pieces.append(jnp.where(lo_half, first, second))
        o_ref[...] = jnp.concatenate(pieces, axis=1)


def _pages_per_step(n_pages):
    p = 8
    while n_pages % p:
        p //= 2
    return p


def _dec_indexer(page_table, cache_ikt, slot, iq_rows, w_rows, ik_new, n_sel, name):
    bs, n_pages = page_table.shape
    ts = ik_new.shape[1]
    groups = n_pages // _pages_per_step(n_pages)
    width = _pages_per_step(n_pages) * PAGE_SIZE
    lmat = _strict_lower(width).T

    def page_spec(p):
        return pl.BlockSpec((None, None, IDX_DIM, PAGE_SIZE),
                            lambda b, pt: (slot, pt[b, p], 0, 0))

    grid_spec = pltpu.PrefetchScalarGridSpec(
        num_scalar_prefetch=1,
        grid=(bs,),
        in_specs=[
            pl.BlockSpec((None, IDX_HEADS * ts, IDX_DIM), lambda b, pt: (b, 0, 0)),
            pl.BlockSpec((None, IDX_HEADS * ts, 1), lambda b, pt: (b, 0, 0)),
            pl.BlockSpec((None, ts, IDX_DIM), lambda b, pt: (b, 0, 0)),
            pl.BlockSpec((width, width), lambda b, pt: (0, 0)),
        ] + [page_spec(p) for p in range(n_pages)],
        out_specs=pl.BlockSpec((None, groups + 1, ts, width), lambda b, pt: (b, 0, 0, 0)),
        scratch_shapes=[pltpu.VMEM((groups + 1, ts, width), I32)],
    )
    kern = functools.partial(_dec_idx_kernel, pages=n_pages, groups=groups, n_sel=float(n_sel))
    return pl.pallas_call(
        kern,
        grid_spec=grid_spec,
        out_shape=jax.ShapeDtypeStruct((bs, groups + 1, ts, width), F32),
        compiler_params=_cparams(("parallel",), 32),
        name=name,
    )(page_table, iq_rows, w_rows, ik_new, lmat, *([cache_ikt] * n_pages))


def _dec_mixer(mode, layer, page_table, cache_kt, cache_vt, qbd, k_new, v_new, *, extra, name):
    bs, n_pages = page_table.shape
    ts = k_new.shape[1]
    pages = _pages_per_step(n_pages)
    groups = n_pages // pages
    width = pages * PAGE_SIZE
    rows = 2 * N_PAIRS * ts
    descending = mode == "b"

    def page_spec(p):
        def imap(b, g, pt):
            grp = groups - 1 - g if descending else g
            return (layer, pt[b, grp * pages + p], 0, 0)
        return pl.BlockSpec((None, None, TOK_WIDTH, PAGE_SIZE), imap)

    in_specs = [
        pl.BlockSpec((None, rows, TOK_WIDTH), lambda b, g, pt: (b, 0, 0)),
        pl.BlockSpec((None, ts, TOK_WIDTH), lambda b, g, pt: (b, 0, 0)),
        pl.BlockSpec((None, ts, TOK_WIDTH), lambda b, g, pt: (b, 0, 0)),
    ]
    args = [qbd, k_new, v_new]
    lam_init = 0.0
    if mode == "a":
        in_specs += [pl.BlockSpec((None, None, ts, width), lambda b, g, pt: (b, 0, 0, 0)),
                     pl.BlockSpec((None, None, ts, width), lambda b, g, pt: (b, g + 1, 0, 0))]
        args += [extra, extra]
    elif mode == "b":
        in_specs.append(pl.BlockSpec((width, width), lambda b, g, pt: (0, 0)))
        args.append(_strict_lower(width))
    else:
        lam_p, norm_g, lam_init = extra
        in_specs += [pl.BlockSpec(lam_p.shape, lambda b, g, pt: (0, 0)),
                     pl.BlockSpec((1, LANES), lambda b, g, pt: (0, 0))]
        args += [lam_p, norm_g.reshape(1, LANES)]
    in_specs += [page_spec(p) for p in range(pages)] * 2
    args += [cache_kt] * pages + [cache_vt] * pages
    grid_spec = pltpu.PrefetchScalarGridSpec(
        num_scalar_prefetch=1,
        grid=(bs, groups),
        in_specs=in_specs,
        out_specs=pl.BlockSpec((None, ts, TOK_WIDTH), lambda b, g, pt: (b, 0, 0)),
        scratch_shapes=[pltpu.VMEM((rows, 1), F32), pltpu.VMEM((rows, 1), F32),
                        pltpu.VMEM((rows, TOK_WIDTH), F32)],
    )
    kern = functools.partial(_dec_attn_kernel, mode=mode, pages=pages, groups=groups, ts=ts,
                             lam_init=lam_init)
    return pl.pallas_call(
        kern,
        grid_spec=grid_spec,
        out_shape=jax.ShapeDtypeStruct((bs, ts, TOK_WIDTH), F32),
        compiler_params=_cparams(("parallel", "arbitrary"), 48),
        name=name,
    )(page_table, *args)


def _oproj_ln_kernel(tok_ref, mem_ref, x_ref, wt_ref, wm_ref, g_ref, b_ref, o_ref, obf_ref, *, alpha):
    mix = (_dot(tok_ref[...].astype(BF16), wt_ref[...])
           + _dot(mem_ref[...].astype(BF16), wm_ref[...]))
    out = _layer_norm(alpha * x_ref[...] + mix, g_ref[...], b_ref[...])
    o_ref[...] = out
    obf_ref[...] = out.astype(BF16)


def _oproj_ln(tok, mem, x, w_tok, w_mem, g, b, alpha, name):
    m_rows, d = x.shape
    tm = _tiles(m_rows)["proj"]
    row = lambda n: pl.BlockSpec((tm, n), lambda i: (i, 0))
    full = lambda a: pl.BlockSpec(a.shape, lambda i: (0, 0))
    return pl.pallas_call(
        functools.partial(_oproj_ln_kernel, alpha=alpha),
        grid=(m_rows // tm,),
        in_specs=[row(TOK_WIDTH), row(MEM_WIDTH), row(d), full(w_tok), full(w_mem), full(g), full(b)],
        out_specs=[row(d), row(d)],
        out_shape=[jax.ShapeDtypeStruct((m_rows, d), F32), jax.ShapeDtypeStruct((m_rows, d), BF16)],
        compiler_params=_cparams(("parallel",), 40),
        name=name,
    )(tok, mem, x, w_tok, w_mem, g, b)


def _ffn_ln_kernel(xbf_ref, x_ref, wg_ref, wu_ref, wd_ref, g_ref, b_ref, o_ref, obf_ref,
                   *, alpha, halves):
    rows = xbf_ref.shape[0] // halves
    for r in range(halves):
        sl = slice(r * rows, (r + 1) * rows)
        xb = xbf_ref[sl, :]
        hg = _dot(xb, wg_ref[...])
        hu = _dot(xb, wu_ref[...])
        act = hg * (1.0 / (1.0 + jnp.exp(-hg))) * hu
        y = _dot(act.astype(BF16), wd_ref[...])
        out = _layer_norm(alpha * x_ref[sl, :] + y, g_ref[...], b_ref[...])
        o_ref[sl, :] = out
        obf_ref[sl, :] = out.astype(BF16)


def _ffn_ln(xbf, x, w_gu, w_d, g, b, alpha, name):
    m_rows, d = x.shape
    d_ff = w_d.shape[0]
    tm = _tiles(m_rows)["ffn"]
    halves = 2
    row = lambda: pl.BlockSpec((tm, d), lambda i: (i, 0))
    vec = lambda a: pl.BlockSpec(a.shape, lambda i: (0, 0))
    once = pl.Buffered(1)
    return pl.pallas_call(
        functools.partial(_ffn_ln_kernel, alpha=alpha, halves=halves),
        grid=(m_rows // tm,),
        in_specs=[row(), row(),
                  pl.BlockSpec((d, d_ff), lambda i: (0, 0), pipeline_mode=once),
                  pl.BlockSpec((d, d_ff), lambda i: (0, 1), pipeline_mode=once),
                  pl.BlockSpec((d_ff, d), lambda i: (0, 0), pipeline_mode=once),
                  vec(g), vec(b)],
        out_specs=[row(), row()],
        out_shape=[jax.ShapeDtypeStruct((m_rows, d), F32), jax.ShapeDtypeStruct((m_rows, d), BF16)],
        compiler_params=_cparams(("parallel",), 56),
        name=name,
    )(xbf, x, w_gu, w_gu, w_d, g, b)


def _rope_tables(pos):
    half = HEAD_DIM // 2
    inv = ROPE_THETA ** (-jnp.arange(half, dtype=F32) / half)
    ang = pos.astype(F32)[:, None] * inv[None, :]
    cos = jnp.cos(ang)
    sin = jnp.sin(ang)
    return jnp.tile(cos, (1, 4)), jnp.tile(jnp.concatenate([-sin, sin], axis=1), (1, 2))


def _heads_last(xt, n_heads):
    l, b, _, p = xt.shape
    return xt.reshape(l, b, n_heads, HEAD_DIM, p).transpose(0, 1, 4, 2, 3)


def kernel(x_prompt, x_sample, mem_prompt, cache_k, cache_v, cache_idx_k, cache_mem_k, cache_mem_v, page_table, w_in, w_idx, w_mem_kv, w_o, ln_mix_g, ln_mix_b, w_gate_up, w_down, ln_ffn_g, ln_ffn_b, diff_lambda, diff_norm_g):
    bp, tp, d_model = x_prompt.shape
    bs, ts, _ = x_sample.shape
    n_pages = page_table.shape[1]
    past = n_pages * PAGE_SIZE
    depth = w_in.shape[0]
    n_mem = mem_prompt.shape[1]
    n_pool = cache_k.shape[1]
    n_heads = 2 * N_PAIRS
    mem_heads = MEM_WIDTH // HEAD_DIM
    alpha = (2 * depth) ** 0.25
    nsel_p = min(TOPK_MAX, tp // 4)
    nsel_s = min(TOPK_MAX, (past + ts) // 4)
    mp, ms = bp * tp, bs * ts

    w_in_b = w_in.astype(BF16)
    w_mkv_b = w_mem_kv.astype(BF16)
    w_o_b = w_o.astype(BF16)
    w_gu_b = w_gate_up.astype(BF16)
    w_d_b = w_down.astype(BF16)
    iq_w, ik_w, ih_w = (w_idx[..., :IDX_Q_WIDTH], w_idx[..., IDX_Q_WIDTH:IDX_Q_WIDTH + IDX_DIM],
                        w_idx[..., IDX_Q_WIDTH + IDX_DIM:])
    w_idx_b = jnp.concatenate(
        [iq_w, ik_w, ik_w, ih_w, jnp.zeros(ih_w.shape[:2] + (LANES - IDX_HEADS,), F32)],
        axis=-1).astype(BF16)
    idx_cols = IDX_Q_WIDTH + 2 * LANES

    rope_p = _rope_tables(jnp.arange(tp))
    rope_s = _rope_tables(jnp.tile(past + jnp.arange(ts), bs))
    cache_kt = cache_k.transpose(0, 1, 3, 4, 2).reshape(depth, n_pool, TOK_WIDTH, PAGE_SIZE)
    cache_vt = cache_v.transpose(0, 1, 3, 4, 2).reshape(depth, n_pool, TOK_WIDTH, PAGE_SIZE)
    cache_ikt = cache_idx_k.transpose(0, 1, 3, 2)
    cache_mkt = cache_mem_k.transpose(0, 1, 3, 4, 2).reshape(depth, bs, MEM_WIDTH, n_mem)
    cache_mvt = cache_mem_v.transpose(0, 1, 3, 4, 2).reshape(depth, bs, MEM_WIDTH, n_mem)
    mem2d = mem_prompt.reshape(bp * n_mem, d_model)
    head_of_col = jnp.arange(TOK_WIDTH) // HEAD_DIM
    bd_mask = (jnp.arange(n_heads)[:, None, None] == head_of_col[None, None, :])

    xp = x_prompt.reshape(mp, d_model)
    xs = x_sample.reshape(ms, d_model)
    xp_in, xs_in = xp, xs
    nk_p, nv_p, nidx_p, nmk_p, nmv_p, nk_s, nv_s, nidx_s = [], [], [], [], [], [], [], []

    for l in range(depth):
        kind = l % N_MIXERS
        slot = l // N_MIXERS
        n_rope = n_heads if kind != 1 else 0
        tag = f"l{l}"
        c_q, c_k, c_v, c_m = 0, TOK_WIDTH, 2 * TOK_WIDTH, 3 * TOK_WIDTH
        in_w = c_m + MEM_WIDTH

        q_p, kt_f, kt_p, vt_f, v_p, mq_p = _proj(
            xp_in, w_in_b[l],
            (("pair", c_q, c_k, QK_SCALE, BF16), ("T", c_k, c_v, 1.0, F32),
             ("Ttile", c_k, c_v, 1.0, BF16), ("T", c_v, c_m, 1.0, F32),
             ("pair", c_v, c_m, 1.0, BF16), ("flat", c_m, in_w, QK_SCALE, BF16)),
            batch=bp, rope=rope_p, n_rope_chunks=n_rope, name=f"proj_p_{tag}")
        nk_p.append(kt_f)
        nv_p.append(vt_f)
        mkt_f, mvt_f = _proj(
            mem2d, w_mkv_b[l],
            (("T", 0, MEM_WIDTH, 1.0, F32), ("T", MEM_WIDTH, 2 * MEM_WIDTH, 1.0, F32)),
            batch=bp, name=f"proj_mem_{tag}")
        nmk_p.append(mkt_f)
        nmv_p.append(mvt_f)
        mem_p = _mem_attn(mq_p.reshape(bp, tp, MEM_WIDTH), mkt_f, mvt_f, None, BF16, f"mem_p_{tag}")

        q_s, k_s, v_s, mq_s = _proj(
            xs_in, w_in_b[l],
            (("flat", c_q, c_k, QK_SCALE, F32), ("flat", c_k, c_v, 1.0, F32),
             ("flat", c_v, c_m, 1.0, F32), ("flat", c_m, in_w, QK_SCALE, F32)),
            batch=bs, rope=rope_s, n_rope_chunks=n_rope, name=f"proj_s_{tag}")
        k_s3 = k_s.reshape(bs, ts, TOK_WIDTH)
        v_s3 = v_s.reshape(bs, ts, TOK_WIDTH)
        nk_s.append(k_s3.reshape(bs, ts, n_heads, HEAD_DIM))
        nv_s.append(v_s3.reshape(bs, ts, n_heads, HEAD_DIM))
        qbd = jnp.where(bd_mask[None], q_s.reshape(bs, 1, ts, TOK_WIDTH), 0.0)
        qbd = qbd.reshape(bs, n_heads * ts, TOK_WIDTH).astype(BF16)
        mem_s = _mem_attn(mq_s.reshape(bs, ts, MEM_WIDTH), cache_mkt, cache_mvt, l, F32,
                          f"mem_s_{tag}")

        if kind == 0:
            iq_p, ikt_p, ikt_f, iw_p = _proj(
                xp_in, w_idx_b[slot],
                (("flat", 0, IDX_Q_WIDTH, 1.0, BF16),
                 ("Ttile", IDX_Q_WIDTH, IDX_Q_WIDTH + LANES, 1.0, BF16),
                 ("T", IDX_Q_WIDTH, IDX_Q_WIDTH + IDX_DIM, 1.0, F32),
                 ("flat", IDX_Q_WIDTH + LANES, idx_cols, IDX_W_SCALE, F32)),
                batch=bp, rope=rope_p, n_rope_chunks=IDX_Q_WIDTH // LANES + 1, name=f"proj_ip_{tag}")
            nidx_p.append(ikt_f)
            tok_p = _prompt_mixer(
                0, q_p, kt_p, v_p,
                extra=(iq_p.reshape(bp, tp, IDX_Q_WIDTH), ikt_p, iw_p.reshape(bp, tp, LANES), nsel_p),
                name=f"mix_a_{tag}")

            iq_s, ik_s, iw_s = _proj(
                xs_in, w_idx_b[slot],
                (("flat", 0, IDX_Q_WIDTH, 1.0, F32),
                 ("flat", IDX_Q_WIDTH, IDX_Q_WIDTH + IDX_DIM, 1.0, F32),
                 ("flat", IDX_Q_WIDTH + LANES, idx_cols, IDX_W_SCALE, F32)),
                batch=bs, rope=rope_s, n_rope_chunks=IDX_Q_WIDTH // LANES + 1, name=f"proj_is_{tag}")
            ik_s3 = ik_s.reshape(bs, ts, IDX_DIM)
            nidx_s.append(ik_s3)
            iq_rows = iq_s.reshape(bs, ts, IDX_HEADS, IDX_DIM).swapaxes(1, 2)
            iq_rows = iq_rows.reshape(bs, IDX_HEADS * ts, IDX_DIM).astype(BF16)
            w_rows = iw_s.reshape(bs, ts, LANES)[:, :, :IDX_HEADS].swapaxes(1, 2)
            w_rows = w_rows.reshape(bs, IDX_HEADS * ts, 1)
            bias = _dec_indexer(page_table, cache_ikt, slot, iq_rows, w_rows, ik_s3, nsel_s,
                                f"idx_s_{tag}")
            tok_s = _dec_mixer("a", l, page_table, cache_kt, cache_vt, qbd, k_s3, v_s3,
                               extra=bias, name=f"mix_sa_{tag}")
        elif kind == 1:
            tok_p = _prompt_mixer(1, q_p, kt_p, v_p, extra=None, name=f"mix_b_{tag}")
            tok_s = _dec_mixer("b", l, page_table, cache_kt, cache_vt, qbd, k_s3, v_s3,
                               extra=None, name=f"mix_sb_{tag}")
        else:
            lam_init = 0.8 - 0.6 * math.exp(-0.3 * l)
            extra = (diff_lambda[slot], diff_norm_g[slot], lam_init)
            tok_p = _prompt_mixer(2, q_p, kt_p, v_p, extra=extra, name=f"mix_c_{tag}")
            tok_s = _dec_mixer("c", l, page_table, cache_kt, cache_vt, qbd, k_s3, v_s3,
                               extra=extra, name=f"mix_sc_{tag}")

        w_tok, w_mem = w_o_b[l, :TOK_WIDTH], w_o_b[l, TOK_WIDTH:]
        g1, b1 = ln_mix_g[l].reshape(1, d_model), ln_mix_b[l].reshape(1, d_model)
        g2, b2 = ln_ffn_g[l].reshape(1, d_model), ln_ffn_b[l].reshape(1, d_model)
        x1, x1b = _oproj_ln(tok_p.reshape(mp, TOK_WIDTH), mem_p.reshape(mp, MEM_WIDTH), xp,
                            w_tok, w_mem, g1, b1, alpha, f"oproj_p_{tag}")
        xp, xp_in = _ffn_ln(x1b, x1, w_gu_b[l], w_d_b[l], g2, b2, alpha, f"ffn_p_{tag}")
        y1, y1b = _oproj_ln(tok_s.reshape(ms, TOK_WIDTH), mem_s.reshape(ms, MEM_WIDTH), xs,
                            w_tok, w_mem, g1, b1, alpha, f"oproj_s_{tag}")
        xs, xs_in = _ffn_ln(y1b, y1, w_gu_b[l], w_d_b[l], g2, b2, alpha, f"ffn_s_{tag}")

    return (xp.reshape(bp, tp, d_model), xs.reshape(bs, ts, d_model),
            _heads_last(jnp.stack(nk_p), n_heads), _heads_last(jnp.stack(nv_p), n_heads),
            jnp.stack(nidx_p).transpose(0, 1, 3, 2),
            _heads_last(jnp.stack(nmk_p), mem_heads), _heads_last(jnp.stack(nmv_p), mem_heads),
            jnp.stack(nk_s), jnp.stack(nv_s), jnp.stack(nidx_s))
```
